```python
import math
import jax
import jax.numpy as jnp
from jax import lax
import numpy as np

D_MODEL = 1024
BATCH = 16
SEQ = 2048
DEPTH = 2
DEC_BATCH = 8
DEC_SEQ = 8192
PAST_LEN = 128

HEAD_DIM = 64
N_HEADS = 8
N_KV_HEADS = 2
KV_GROUP = N_HEADS // N_KV_HEADS
ATTN_W = N_HEADS * HEAD_DIM
KV_W = N_KV_HEADS * HEAD_DIM
SSM_GROUP_CH = 16
SSM_W = 256
SSM_GROUPS = SSM_W // SSM_GROUP_CH
SSM_STATE = 64
N_CROSS_HEADS = 4
CROSS_W = N_CROSS_HEADS * HEAD_DIM
N_MEM = 256
N_BRANCHES = 3
GATE_W = N_BRANCHES * D_MODEL
IN_W = ATTN_W + 2 * KV_W + SSM_W + CROSS_W + GATE_W
SPLIT_POINTS = (ATTN_W, ATTN_W + KV_W, ATTN_W + 2 * KV_W, ATTN_W + 2 * KV_W + SSM_W,
                ATTN_W + 2 * KV_W + SSM_W + CROSS_W)
D_FF = 2816
CONV_WIDTH = 3
GRID_W = 64
ROPE_THETA = 10000.0
ROPE_PAIRS = HEAD_DIM // 4
Q_BLOCK = 128
EPS = 1e-6

kernel_name = 'hybrid_gqa_s5_xattn_convffn_encoder'


def rms_norm(x, gain):
    xf = x.astype(jnp.float32)
    y = xf * lax.rsqrt(jnp.mean(xf * xf, axis=-1, keepdims=True) + EPS)
    return (y * gain.astype(jnp.float32)).astype(x.dtype)


def axial_rope_tables(L, dtype):
    rows = L // GRID_W
    r = jnp.broadcast_to(jnp.arange(rows, dtype=jnp.float32)[:, None], (rows, GRID_W)).reshape(L)
    c = jnp.broadcast_to(jnp.arange(GRID_W, dtype=jnp.float32)[None, :], (rows, GRID_W)).reshape(L)
    freqs = ROPE_THETA ** (-jnp.arange(ROPE_PAIRS, dtype=jnp.float32) / ROPE_PAIRS)
    ang_r = r[:, None] * freqs
    ang_c = c[:, None] * freqs
    ang = jnp.concatenate([ang_r, ang_r, ang_c, ang_c], axis=-1)
    return jnp.cos(ang).astype(dtype), jnp.sin(ang).astype(dtype)


def apply_rope(x, cos, sin):
    h = ROPE_PAIRS
    rot = jnp.concatenate([-x[..., h:2 * h], x[..., :h], -x[..., 3 * h:], x[..., 2 * h:3 * h]], axis=-1)
    return x * cos[None, :, None, :] + rot * sin[None, :, None, :]


def self_attention(q, k, v, cos, sin, q_gain, k_gain):
    B, L = q.shape[:2]
    q = apply_rope(rms_norm(q.reshape(B, L, N_HEADS, HEAD_DIM), q_gain), cos, sin)
    k = apply_rope(rms_norm(k.reshape(B, L, N_KV_HEADS, HEAD_DIM), k_gain), cos, sin)
    v = v.reshape(B, L, N_KV_HEADS, HEAD_DIM)
    n_blk = L // Q_BLOCK
    qb = q.reshape(B, n_blk, Q_BLOCK, N_KV_HEADS, KV_GROUP, HEAD_DIM).transpose(1, 0, 2, 3, 4, 5)
    scale = HEAD_DIM ** -0.5

    def attend_block(q_blk):
        s = jnp.einsum('bqhgd,bkhd->bhgqk', q_blk, k).astype(jnp.float32) * scale
        p = jax.nn.softmax(s, axis=-1).astype(v.dtype)
        return jnp.einsum('bhgqk,bkhd->bqhgd', p, v)

    o = lax.map(attend_block, qb)
    return o.transpose(1, 0, 2, 3, 4, 5).reshape(B, L, ATTN_W)


def memory_cross_attention(qc, mem, mem_gain, w_mem_kv):
    B, L = qc.shape[:2]
    m = rms_norm(mem, mem_gain) @ w_mem_kv
    mk, mv = jnp.split(m, 2, axis=-1)
    mk = mk.reshape(B, -1, N_CROSS_HEADS, HEAD_DIM)
    mv = mv.reshape(B, -1, N_CROSS_HEADS, HEAD_DIM)
    q = qc.reshape(B, L, N_CROSS_HEADS, HEAD_DIM)
    s = jnp.einsum('blhd,bmhd->bhlm', q, mk).astype(jnp.float32) * (HEAD_DIM ** -0.5)
    p = jax.nn.softmax(s, axis=-1).astype(mv.dtype)
    return jnp.einsum('bhlm,bmhd->blhd', p, mv).reshape(B, L, CROSS_W)


def _complex_recurrence_combine(e1, e2):
    a1r, a1i, b1r, b1i = e1
    a2r, a2i, b2r, b2i = e2
    return (a2r * a1r - a2i * a1i,
            a2r * a1i + a2i * a1r,
            a2r * b1r - a2i * b1i + b2r,
            a2r * b1i + a2i * b1r + b2i)


def s5_branch(u, a_re, a_im, log_dt, b_re, b_im, c_re, c_im, d_skip, w_glu):
    B, L, _ = u.shape
    f32 = jnp.float32
    uf = u.astype(f32)
    ug = uf.reshape(B, L, SSM_GROUPS, SSM_GROUP_CH)
    bu_re = jnp.einsum('blgh,gph->blgp', ug, b_re.astype(f32))
    bu_im = jnp.einsum('blgh,gph->blgp', ug, b_im.astype(f32))
    s_re = jnp.zeros_like(bu_re)
    s_im = jnp.zeros_like(bu_im)
    for direction in range(2):
        ar = a_re[direction].astype(f32)
        ai = a_im[direction].astype(f32)
        dt = jnp.exp(log_dt[direction].astype(f32))[:, None]
        mag = jnp.exp(ar * dt)
        lam_re = mag * jnp.cos(ai * dt)
        lam_im = mag * jnp.sin(ai * dt)
        num_re = lam_re - 1.0
        den = ar * ar + ai * ai
        coef_re = (num_re * ar + lam_im * ai) / den
        coef_im = (lam_im * ar - num_re * ai) / den
        xin_re = coef_re * bu_re - coef_im * bu_im
        xin_im = coef_re * bu_im + coef_im * bu_re
        if direction == 1:
            xin_re = jnp.flip(xin_re, axis=1)
            xin_im = jnp.flip(xin_im, axis=1)
        elems = (jnp.broadcast_to(lam_re, xin_re.shape), jnp.broadcast_to(lam_im, xin_im.shape), xin_re, xin_im)
        _, _, h_re, h_im = lax.associative_scan(_complex_recurrence_combine, elems, axis=1)
        if direction == 1:
            h_re = jnp.flip(h_re, axis=1)
            h_im = jnp.flip(h_im, axis=1)
        s_re = s_re + h_re
        s_im = s_im + h_im
    y = (jnp.einsum('blgp,ghp->blgh', s_re, c_re.astype(f32))
         - jnp.einsum('blgp,ghp->blgh', s_im, c_im.astype(f32)))
    y = y.reshape(B, L, SSM_W) + d_skip.astype(f32) * uf
    y = jax.nn.gelu(y)
    y = y * jax.nn.sigmoid(y @ w_glu.astype(f32))
    return y.astype(u.dtype)


def conv_gated_ffn(h, w_up, conv_w, conv_b, w_down):
    L = h.shape[1]
    g, val = jnp.split(h @ w_up, 2, axis=-1)
    gp = jnp.pad(g, ((0, 0), (1, 1), (0, 0)))
    g = gp[:, :L] * conv_w[0] + gp[:, 1:L + 1] * conv_w[1] + gp[:, 2:] * conv_w[2] + conv_b
    return (jax.nn.gelu(g) * val) @ w_down


def encode(x, mem, p):
    L = x.shape[1]
    cos, sin = axial_rope_tables(L, x.dtype)
    for l in range(DEPTH):
        h = rms_norm(x, p['norm_mix'][l])
        z = h @ p['w_in'][l]
        q, k, v, u, qc, gates = jnp.split(z, SPLIT_POINTS, axis=-1)
        a = self_attention(q, k, v, cos, sin, p['q_norm'][l], p['k_norm'][l]) @ p['p_attn'][l]
        s = s5_branch(u, p['ssm_a_re'][l], p['ssm_a_im'][l], p['ssm_log_dt'][l],
                      p['ssm_b_re'][l], p['ssm_b_im'][l], p['ssm_c_re'][l], p['ssm_c_im'][l],
                      p['ssm_d'][l], p['ssm_glu'][l]) @ p['p_ssm'][l]
        c = memory_cross_attention(qc, mem, p['mem_norm'][l], p['w_mem_kv'][l]) @ p['p_cross'][l]
        g_a, g_s, g_c = jnp.split(jax.nn.sigmoid(gates), N_BRANCHES, axis=-1)
        x = x + (g_a * a + g_s * s + g_c * c) @ p['w_out'][l]
        x = x + conv_gated_ffn(rms_norm(x, p['norm_ffn'][l]), p['w_up'][l], p['conv_w'][l],
                               p['conv_b'][l], p['w_down'][l])
    return rms_norm(x, p['norm_final'])


def setup_inputs(seed: int = 0) -> dict:
    key = jax.random.key(seed)
    ks = iter(jax.random.split(key, 40))
    f32 = jnp.float32

    def nrm(shape, scale):
        return jax.random.normal(next(ks), shape, f32) * scale

    def gain(shape):
        return 1.0 + 0.02 * jax.random.normal(next(ks), shape, f32)

    n_idx = jnp.arange(SSM_STATE, dtype=f32)
    ssm_shape = (DEPTH, 2, SSM_GROUPS, SSM_STATE)
    return {
        'x_prompt': nrm((BATCH, SEQ, D_MODEL), 1.0),
        'x_sample': nrm((DEC_BATCH, DEC_SEQ, D_MODEL), 1.0),
        'mem_prompt': nrm((BATCH, N_MEM, D_MODEL), 1.0),
        'mem_sample': nrm((DEC_BATCH, N_MEM, D_MODEL), 1.0),
        'norm_mix': gain((DEPTH, D_MODEL)),
        'w_in': nrm((DEPTH, D_MODEL, IN_W), D_MODEL ** -0.5),
        'q_norm': gain((DEPTH, HEAD_DIM)),
        'k_norm': gain((DEPTH, HEAD_DIM)),
        'ssm_a_re': -0.5 * jnp.exp(nrm(ssm_shape, 0.02)),
        'ssm_a_im': math.pi * n_idx + nrm(ssm_shape, 0.01),
        'ssm_log_dt': jax.random.uniform(next(ks), (DEPTH, 2, SSM_GROUPS), f32,
                                         minval=math.log(1e-3), maxval=math.log(1e-1)),
        'ssm_b_re': nrm((DEPTH, SSM_GROUPS, SSM_STATE, SSM_GROUP_CH), (2 * SSM_GROUP_CH) ** -0.5),
        'ssm_b_im': nrm((DEPTH, SSM_GROUPS, SSM_STATE, SSM_GROUP_CH), (2 * SSM_GROUP_CH) ** -0.5),
        'ssm_c_re': nrm((DEPTH, SSM_GROUPS, SSM_GROUP_CH, SSM_STATE), SSM_STATE ** -0.5),
        'ssm_c_im': nrm((DEPTH, SSM_GROUPS, SSM_GROUP_CH, SSM_STATE), SSM_STATE ** -0.5),
        'ssm_d': nrm((DEPTH, SSM_W), 1.0),
        'ssm_glu': nrm((DEPTH, SSM_W, SSM_W), SSM_W ** -0.5),
        'mem_norm': gain((DEPTH, D_MODEL)),
        'w_mem_kv': nrm((DEPTH, D_MODEL, 2 * CROSS_W), D_MODEL ** -0.5),
        'p_attn': nrm((DEPTH, ATTN_W, D_MODEL), ATTN_W ** -0.5),
        'p_ssm': nrm((DEPTH, SSM_W, D_MODEL), SSM_W ** -0.5),
        'p_cross': nrm((DEPTH, CROSS_W, D_MODEL), CROSS_W ** -0.5),
        'w_out': nrm((DEPTH, D_MODEL, D_MODEL), D_MODEL ** -0.5),
        'norm_ffn': gain((DEPTH, D_MODEL)),
        'w_up': nrm((DEPTH, D_MODEL, 2 * D_FF), D_MODEL ** -0.5),
        'conv_w': nrm((DEPTH, CONV_WIDTH, D_FF), CONV_WIDTH ** -0.5),
        'conv_b': nrm((DEPTH, D_FF), 0.01),
        'w_down': nrm((DEPTH, D_FF, D_MODEL), D_FF ** -0.5),
        'norm_final': gain((D_MODEL,)),
    }


def reference(x_prompt, x_sample, mem_prompt, mem_sample, norm_mix, w_in, q_norm, k_norm,
              ssm_a_re, ssm_a_im, ssm_log_dt, ssm_b_re, ssm_b_im, ssm_c_re, ssm_c_im, ssm_d, ssm_glu,
              mem_norm, w_mem_kv, p_attn, p_ssm, p_cross, w_out, norm_ffn, w_up, conv_w, conv_b,
              w_down, norm_final):
    params = dict(norm_mix=norm_mix, w_in=w_in, q_norm=q_norm, k_norm=k_norm,
                  ssm_a_re=ssm_a_re, ssm_a_im=ssm_a_im, ssm_log_dt=ssm_log_dt,
                  ssm_b_re=ssm_b_re, ssm_b_im=ssm_b_im, ssm_c_re=ssm_c_re, ssm_c_im=ssm_c_im,
                  ssm_d=ssm_d, ssm_glu=ssm_glu, mem_norm=mem_norm, w_mem_kv=w_mem_kv,
                  p_attn=p_attn, p_ssm=p_ssm, p_cross=p_cross, w_out=w_out, norm_ffn=norm_ffn,
                  w_up=w_up, conv_w=conv_w, conv_b=conv_b, w_down=w_down, norm_final=norm_final)
    y_prompt = encode(x_prompt, mem_prompt, params)
    y_sample = encode(x_sample, mem_sample, params)
    return (y_prompt, y_sample)
```

```python
import functools
import math

import jax
import jax.numpy as jnp
from jax import lax
from jax.experimental import pallas as pl
from jax.experimental.pallas import tpu as pltpu

HEAD_DIM = 64
N_HEADS = 8
N_KV_HEADS = 2
KV_GROUP = N_HEADS // N_KV_HEADS
ATTN_W = N_HEADS * HEAD_DIM
KV_W = N_KV_HEADS * HEAD_DIM
SSM_GROUP_CH = 16
SSM_W = 256
SSM_GROUPS = SSM_W // SSM_GROUP_CH
SSM_STATE = 64
N_CROSS_HEADS = 4
CROSS_W = N_CROSS_HEADS * HEAD_DIM
N_BRANCHES = 3
GRID_W = 64
ROPE_THETA = 10000.0
ROPE_PAIRS = HEAD_DIM // 4
EPS = 1e-6

LANES = 128
HALO = 16
VMEM_LIMIT = 56 * 1024 * 1024

F32 = jnp.float32
BF16 = jnp.bfloat16


def _cparams(*sem):
    return pltpu.CompilerParams(dimension_semantics=sem, vmem_limit_bytes=VMEM_LIMIT)


def _const_spec(shape):
    nd = len(shape)
    return pl.BlockSpec(shape, lambda *_: (0,) * nd, pipeline_mode=pl.Buffered(1))


def _rms(x, gain):
    return x * lax.rsqrt(jnp.mean(x * x, axis=-1, keepdims=True) + EPS) * gain


def _pick(n, pref):
    t = min(n, pref)
    while n % t:
        t //= 2
    return t


def _head_norm_rope(x, gain, cos, sin_up, sin_dn, out_scale):
    w = x.shape[1]
    reps = w // LANES
    tile = lambda a: a if reps == 1 else jnp.concatenate([a] * reps, axis=1)
    xg = x * gain
    y = (xg * tile(cos)
         + pltpu.roll(xg, w - ROPE_PAIRS, 1) * tile(sin_up)
         + pltpu.roll(xg, ROPE_PAIRS, 1) * tile(sin_dn))
    lane = lax.broadcasted_iota(jnp.int32, (1, LANES), 1)
    low = lane < HEAD_DIM
    outs = []
    for p in range(reps):
        xp = x[:, p * LANES:(p + 1) * LANES]
        sq = xp * xp
        ss_all = jnp.sum(sq, axis=-1, keepdims=True)
        ss_lo = jnp.sum(jnp.where(low, sq, 0.0), axis=-1, keepdims=True)
        r_lo = lax.rsqrt(ss_lo * (1.0 / HEAD_DIM) + EPS) * out_scale
        r_hi = lax.rsqrt((ss_all - ss_lo) * (1.0 / HEAD_DIM) + EPS) * out_scale
        outs.append(y[:, p * LANES:(p + 1) * LANES] * jnp.where(low, r_lo, r_hi))
    return outs[0] if reps == 1 else jnp.concatenate(outs, axis=1)


def _in_proj_kernel(x_ref, g_ref, w_ref, cos_ref, sup_ref, sdn_ref, qg_ref, kg_ref,
                    q_ref, kt_ref, v_ref, u_ref, qc_ref, gate_ref):
    h = _rms(x_ref[0], g_ref[...]).astype(BF16)

    def seg(a, b):
        return jnp.dot(h, w_ref[:, a:b], preferred_element_type=F32)

    cos, sup, sdn = cos_ref[...], sup_ref[...], sdn_ref[...]
    o = 0
    q = seg(o, o + ATTN_W)
    q_ref[0] = _head_norm_rope(q, qg_ref[...], cos, sup, sdn, HEAD_DIM ** -0.5).astype(BF16)
    o += ATTN_W
    k = _head_norm_rope(seg(o, o + KV_W), kg_ref[...], cos, sup, sdn, 1.0)
    kt_ref[0, 0] = k.T.astype(BF16)
    o += KV_W
    v_ref[0] = seg(o, o + KV_W).astype(BF16)
    o += KV_W
    u_ref[0] = seg(o, o + SSM_W).astype(BF16)
    o += SSM_W
    qc_ref[0] = (seg(o, o + CROSS_W) * HEAD_DIM ** -0.5).astype(BF16)
    o += CROSS_W
    d = x_ref.shape[2]
    for j in range(N_BRANCHES):
        gate_ref[0, :, j * d:(j + 1) * d] = jax.nn.sigmoid(seg(o + j * d, o + (j + 1) * d)).astype(BF16)


def _in_proj(x, gain, w, tables, q_gain, k_gain, tk):
    b, l, d = x.shape
    n_in = w.shape[1]
    nj = l // tk
    row = lambda width: pl.BlockSpec((1, tk, width), lambda i, j: (i, j, 0))
    tab = pl.BlockSpec((tk, LANES), lambda i, j: (j, 0))
    out_shape = (
        jax.ShapeDtypeStruct((b, l, ATTN_W), BF16),
        jax.ShapeDtypeStruct((b, nj, KV_W, tk), BF16),
        jax.ShapeDtypeStruct((b, l, KV_W), BF16),
        jax.ShapeDtypeStruct((b, l, SSM_W), BF16),
        jax.ShapeDtypeStruct((b, l, CROSS_W), BF16),
        jax.ShapeDtypeStruct((b, l, N_BRANCHES * d), BF16),
    )
    return pl.pallas_call(
        _in_proj_kernel,
        grid=(b, nj),
        in_specs=[row(d), _const_spec((1, d)), _const_spec((d, n_in)), tab, tab, tab,
                  _const_spec((1, ATTN_W)), _const_spec((1, KV_W))],
        out_specs=(row(ATTN_W), pl.BlockSpec((1, 1, KV_W, tk), lambda i, j: (i, j, 0, 0)),
                   row(KV_W), row(SSM_W), row(CROSS_W), row(N_BRANCHES * d)),
        out_shape=out_shape,
        compiler_params=_cparams("parallel", "parallel"),
        name="in_proj",
    )(x, gain, w, *tables, q_gain, k_gain)


def _head_lanes(pair, h):
    return pair[:, (h % 2) * HEAD_DIM:(h % 2 + 1) * HEAD_DIM]


def _attn_kernel(q_ref, kt_ref, v_ref, o_ref):
    tq = q_ref.shape[1]
    nc, tk = kt_ref.shape[1], kt_ref.shape[3]
    outs = []
    for h in range(N_HEADS):
        g = h // KV_GROUP
        qh = _head_lanes(q_ref[0, :, (h // 2) * LANES:(h // 2 + 1) * LANES], h)

        def body(c, carry):
            m, l, acc = carry
            kt = kt_ref[0, c, g * HEAD_DIM:(g + 1) * HEAD_DIM, :]
            s = jnp.dot(qh, kt, preferred_element_type=F32)
            m_new = jnp.maximum(m, jnp.max(s, axis=-1, keepdims=True))
            alpha = jnp.exp(m - m_new)
            p = jnp.exp(s - m_new)
            l = alpha * l + jnp.sum(p, axis=-1, keepdims=True)
            vv = v_ref[0, pl.ds(pl.multiple_of(c * tk, tk), tk), :]
            acc = alpha * acc + jnp.dot(p.astype(BF16), vv, preferred_element_type=F32)
            return m_new, l, acc

        init = (jnp.full((tq, 1), -jnp.inf, F32), jnp.zeros((tq, 1), F32), jnp.zeros((tq, KV_W), F32))
        _, l, acc = lax.fori_loop(0, nc, body, init)
        outs.append(acc[:, g * HEAD_DIM:(g + 1) * HEAD_DIM] / l)
    o_ref[0] = jnp.concatenate(outs, axis=1).astype(BF16)


def _attention(q, kt, v, tq):
    b, l, _ = q.shape
    nc, tk = kt.shape[1], kt.shape[3]
    return pl.pallas_call(
        _attn_kernel,
        grid=(b, l // tq),
        in_specs=[pl.BlockSpec((1, tq, ATTN_W), lambda i, j: (i, j, 0)),
                  pl.BlockSpec((1, nc, KV_W, tk), lambda i, j: (i, 0, 0, 0)),
                  pl.BlockSpec((1, l, KV_W), lambda i, j: (i, 0, 0))],
        out_specs=pl.BlockSpec((1, tq, ATTN_W), lambda i, j: (i, j, 0)),
        out_shape=jax.ShapeDtypeStruct((b, l, ATTN_W), BF16),
        compiler_params=_cparams("parallel", "parallel"),
        name="attention",
    )(q, kt, v)


def _complex_powers(lr, li, n):
    pr, pi = jnp.ones_like(lr)[None], jnp.zeros_like(li)[None]
    cr, ci = lr, li
    while pr.shape[0] < n:
        pr, pi = (jnp.concatenate([pr, pr * cr - pi * ci]), jnp.concatenate([pi, pr * ci + pi * cr]))
        cr, ci = cr * cr - ci * ci, 2.0 * cr * ci
    return pr[:n], pi[:n]


def _s5_operators(a_re, a_im, log_dt, b_re, b_im, c_re, c_im, t):
    hi = lax.Precision.HIGHEST
    a_re, a_im, log_dt = a_re.astype(F32), a_im.astype(F32), log_dt.astype(F32)
    b_re, b_im, c_re, c_im = (z.astype(F32) for z in (b_re, b_im, c_re, c_im))
    dt = jnp.exp(log_dt)[..., None]
    mag = jnp.exp(a_re * dt)
    lam_re, lam_im = mag * jnp.cos(a_im * dt), mag * jnp.sin(a_im * dt)
    num_re = lam_re - 1.0
    den = a_re * a_re + a_im * a_im
    coef_re = (num_re * a_re + lam_im * a_im) / den
    coef_im = (lam_im * a_re - num_re * a_im) / den
    pw_re, pw_im = _complex_powers(lam_re, lam_im, t + 1)
    e_re = pw_re * coef_re - pw_im * coef_im
    e_im = pw_re * coef_im + pw_im * coef_re
    g, p, hch = b_re.shape

    cb_re = c_re[:, :, :, None] * b_re[:, None, :, :] - c_im[:, :, :, None] * b_im[:, None, :, :]
    cb_im = c_re[:, :, :, None] * b_im[:, None, :, :] + c_im[:, :, :, None] * b_re[:, None, :, :]
    kern = (jnp.einsum('tdgp,ghpk->dgthk', e_re[:t], cb_re, precision=hi)
            - jnp.einsum('tdgp,ghpk->dgthk', e_im[:t], cb_im, precision=hi))
    k_f, k_b = kern[0], kern[1]
    k_cat = jnp.concatenate([k_b[:, :0:-1], (k_f[:, :1] + k_b[:, :1]), k_f[:, 1:]], axis=1)
    idx = jnp.arange(t)[None, :] - jnp.arange(t)[:, None] + (t - 1)
    toep = k_cat[:, idx]
    toep = toep.transpose(0, 1, 4, 2, 3).reshape(g, t * hch, t * hch)

    def times_b(er, ei):
        re = er[:, :, :, None] * b_re[None] - ei[:, :, :, None] * b_im[None]
        im = er[:, :, :, None] * b_im[None] + ei[:, :, :, None] * b_re[None]
        f = lambda z: z.transpose(1, 0, 3, 2).reshape(g, t * hch, p)
        return f(re), f(im)
    sf_re, sf_im = times_b(e_re[:t, 0][::-1], e_im[:t, 0][::-1])
    sb_re, sb_im = times_b(e_re[:t, 1], e_im[:t, 1])
    w_s = jnp.concatenate([sf_re, sf_im, sb_re, sb_im], axis=2)

    def c_times(pr, pi):
        re = c_re[None] * pr[:, :, None, :] - c_im[None] * pi[:, :, None, :]
        im = c_re[None] * pi[:, :, None, :] + c_im[None] * pr[:, :, None, :]
        f = lambda z: z.transpose(1, 3, 0, 2).reshape(g, p, t * hch)
        return f(re), f(-im)
    of_re, of_im = c_times(pw_re[1:t + 1, 0], pw_im[1:t + 1, 0])
    ob_re, ob_im = c_times(pw_re[1:t + 1, 1][::-1], pw_im[1:t + 1, 1][::-1])
    w_o = jnp.concatenate([of_re, of_im, ob_re, ob_im], axis=1)

    lt_re, lt_im = pw_re[t], pw_im[t]
    a1 = jnp.concatenate([lt_re[0], lt_re[0], lt_re[1], lt_re[1]], axis=-1)[:, None, :]
    a2 = jnp.concatenate([-lt_im[0], lt_im[0], -lt_im[1], lt_im[1]], axis=-1)[:, None, :]
    return toep.astype(BF16), w_s.astype(BF16), w_o.astype(BF16), a1, a2


def _s5_state_kernel(u_ref, ws_ref, s_ref):
    s_ref[0] = jnp.dot(u_ref[0], ws_ref[0], preferred_element_type=F32)


def _s5_state(ut, w_s, tr):
    g, r, k = ut.shape
    n = w_s.shape[2]
    return pl.pallas_call(
        _s5_state_kernel,
        grid=(g, r // tr),
        in_specs=[pl.BlockSpec((1, tr, k), lambda i, j: (i, j, 0)),
                  pl.BlockSpec((1, k, n), lambda i, j: (i, 0, 0))],
        out_specs=pl.BlockSpec((1, tr, n), lambda i, j: (i, j, 0)),
        out_shape=jax.ShapeDtypeStruct((g, r, n), F32),
        compiler_params=_cparams("parallel", "parallel"),
        name="s5_state",
    )(ut, w_s)


def _s5_scan_kernel(s_ref, a1_ref, a2_ref, h_ref, *, nb):
    gb, rows, _ = s_ref.shape
    nchunk = rows // nb
    half = 2 * SSM_STATE
    a1, a2 = a1_ref[...], a2_ref[...]
    a1f, a1b, a2f, a2b = a1[:, :, :half], a1[:, :, half:], a2[:, :, :half], a2[:, :, half:]

    def swap(z):
        return pltpu.roll(z.reshape(gb * nb, half), SSM_STATE, 1).reshape(gb, nb, half)

    def body(j, carry):
        hf, hb = carry
        rf = pl.ds(pl.multiple_of(j * nb, nb), nb)
        rb = pl.ds(pl.multiple_of((nchunk - 1 - j) * nb, nb), nb)
        h_ref[:, rf, :half] = hf
        h_ref[:, rb, half:] = hb
        hf = a1f * hf + a2f * swap(hf) + s_ref[:, rf, :half]
        hb = a1b * hb + a2b * swap(hb) + s_ref[:, rb, half:]
        return hf, hb

    zero = jnp.zeros((gb, nb, half), F32)
    lax.fori_loop(0, nchunk, body, (zero, zero))


def _s5_scan(s, a1, a2, nb, gb):
    g, r, n = s.shape
    blk = pl.BlockSpec((gb, r, n), lambda i: (i, 0, 0))
    par = pl.BlockSpec((gb, 1, n), lambda i: (i, 0, 0))
    return pl.pallas_call(
        functools.partial(_s5_scan_kernel, nb=nb),
        grid=(g // gb,),
        in_specs=[blk, par, par],
        out_specs=blk,
        out_shape=jax.ShapeDtypeStruct((g, r, n), F32),
        compiler_params=_cparams("parallel"),
        name="s5_scan",
    )(s, a1, a2)


def _s5_out_kernel(u_ref, h_ref, toep_ref, wo_ref, y_ref):
    y = jnp.dot(u_ref[0], toep_ref[0], preferred_element_type=F32)
    y = y + jnp.dot(h_ref[0].astype(BF16), wo_ref[0], preferred_element_type=F32)
    y_ref[0] = y.astype(BF16)


def _s5_out(ut, hin, toep, w_o, tr):
    g, r, k = ut.shape
    n = hin.shape[2]
    return pl.pallas_call(
        _s5_out_kernel,
        grid=(g, r // tr),
        in_specs=[pl.BlockSpec((1, tr, k), lambda i, j: (i, j, 0)),
                  pl.BlockSpec((1, tr, n), lambda i, j: (i, j, 0)),
                  pl.BlockSpec((1, k, k), lambda i, j: (i, 0, 0)),
                  pl.BlockSpec((1, n, k), lambda i, j: (i, 0, 0))],
        out_specs=pl.BlockSpec((1, tr, k), lambda i, j: (i, j, 0)),
        out_shape=jax.ShapeDtypeStruct((g, r, k), BF16),
        compiler_params=_cparams("parallel", "parallel"),
        name="s5_out",
    )(ut, hin, toep, w_o)


def _s5_scan_branch(u, ops, t):
    toep, w_s, w_o, a1, a2 = ops
    b, l, _ = u.shape
    c = l // t
    g, hch = SSM_GROUPS, SSM_GROUP_CH
    ut = u.reshape(b, c, t, g, hch).transpose(3, 1, 0, 2, 4).reshape(g, c * b, t * hch)
    tr = _pick(c * b, 512)
    s = _s5_state(ut, w_s, tr)
    hin = _s5_scan(s, a1, a2, b, 4)
    y = _s5_out(ut, hin, toep, w_o, tr)
    return y.reshape(g, c, b, t, hch).transpose(2, 1, 3, 0, 4).reshape(b, l, g * hch)


def _mem_kv_kernel(m_ref, g_ref, w_ref, o_ref):
    h = _rms(m_ref[0], g_ref[...]).astype(BF16)
    o_ref[0] = jnp.dot(h, w_ref[...], preferred_element_type=F32).astype(BF16)


def _mem_kv(mem, gain, w):
    b, m, d = mem.shape
    n = w.shape[1]
    return pl.pallas_call(
        _mem_kv_kernel,
        grid=(b,),
        in_specs=[pl.BlockSpec((1, m, d), lambda i: (i, 0, 0)), _const_spec((1, d)), _const_spec((d, n))],
        out_specs=pl.BlockSpec((1, m, n), lambda i: (i, 0, 0)),
        out_shape=jax.ShapeDtypeStruct((b, m, n), BF16),
        compiler_params=_cparams("parallel"),
        name="mem_kv",
    )(mem, gain, w)


def _merge_kernel(x_ref, a_ref, y_ref, u_ref, qc_ref, gate_ref, mkv_ref, dskip_ref, wglu_ref,
                  pa_ref, ps_ref, pc_ref, wout_ref, o_ref):
    d = x_ref.shape[2]
    mv = mkv_ref[0, :, CROSS_W:]
    heads = []
    for h in range(N_CROSS_HEADS):
        sl = slice(h * HEAD_DIM, (h + 1) * HEAD_DIM)
        pair = slice((h // 2) * LANES, (h // 2 + 1) * LANES)
        s = lax.dot_general(_head_lanes(qc_ref[0, :, pair], h), _head_lanes(mkv_ref[0, :, pair], h),
                            (((1,), (1,)), ((), ())), preferred_element_type=F32)
        p = jnp.exp(s - jnp.max(s, axis=-1, keepdims=True))
        den = jnp.sum(p, axis=-1, keepdims=True)
        o = jnp.dot(p.astype(BF16), mv, preferred_element_type=F32)
        heads.append(o[:, sl] / den)
    cross = jnp.concatenate(heads, axis=1).astype(BF16)
    y = y_ref[0].astype(F32) + dskip_ref[...] * u_ref[0].astype(F32)
    y = jax.nn.gelu(y)
    y = y * jax.nn.sigmoid(jnp.dot(y.astype(BF16), wglu_ref[...], preferred_element_type=F32))
    m = gate_ref[0, :, 0:d].astype(F32) * jnp.dot(a_ref[0], pa_ref[...], preferred_element_type=F32)
    m = m + gate_ref[0, :, d:2 * d].astype(F32) * jnp.dot(y.astype(BF16), ps_ref[...], preferred_element_type=F32)
    m = m + gate_ref[0, :, 2 * d:3 * d].astype(F32) * jnp.dot(cross, pc_ref[...], preferred_element_type=F32)
    o_ref[0] = x_ref[0] + jnp.dot(m.astype(BF16), wout_ref[...], preferred_element_type=F32)


def _merge(x, attn, yscan, u, qc, gates, mkv, d_skip, w_glu, p_attn, p_ssm, p_cross, w_out, tl):
    b, l, d = x.shape
    row = lambda width: pl.BlockSpec((1, tl, width), lambda i, j: (i, j, 0))
    n_mem = mkv.shape[1]
    return pl.pallas_call(
        _merge_kernel,
        grid=(b, l // tl),
        in_specs=[row(d), row(ATTN_W), row(SSM_W), row(SSM_W), row(CROSS_W), row(N_BRANCHES * d),
                  pl.BlockSpec((1, n_mem, 2 * CROSS_W), lambda i, j: (i, 0, 0)),
                  _const_spec((1, SSM_W)), _const_spec((SSM_W, SSM_W)),
                  _const_spec((ATTN_W, d)), _const_spec((SSM_W, d)), _const_spec((CROSS_W, d)),
                  _const_spec((d, d))],
        out_specs=row(d),
        out_shape=jax.ShapeDtypeStruct((b, l, d), F32),
        compiler_params=_cparams("parallel", "parallel"),
        name="merge",
    )(x, attn, yscan, u, qc, gates, mkv, d_skip, w_glu, p_attn, p_ssm, p_cross, w_out)


def _ffn_kernel(xp_ref, x_ref, xn_ref, g_ref, wup_ref, cw_ref, cb_ref, wdn_ref, gf_ref, o_ref,
                h_scr, gate_scr, *, final_norm):
    j = pl.program_id(1)
    nj = pl.num_programs(1)
    tm = x_ref.shape[1]
    dff = wdn_ref.shape[0]
    gain = g_ref[...]
    x = x_ref[0]
    h_scr[0:HALO, :] = _rms(xp_ref[0], gain).astype(BF16)
    h_scr[HALO:HALO + tm, :] = _rms(x, gain).astype(BF16)
    h_scr[HALO + tm:, :] = _rms(xn_ref[0], gain).astype(BF16)
    wc = gate_scr.shape[1]
    y = x
    for c0 in range(0, dff, wc):
        gate_scr[...] = jnp.dot(h_scr[...], wup_ref[:, c0:c0 + wc], preferred_element_type=F32)
        @pl.when(j == 0)
        def _():
            gate_scr[0:HALO, :] = jnp.zeros((HALO, wc), F32)

        @pl.when(j == nj - 1)
        def _():
            gate_scr[HALO + tm:, :] = jnp.zeros((HALO, wc), F32)

        val = jnp.dot(h_scr[HALO:HALO + tm, :], wup_ref[:, dff + c0:dff + c0 + wc], preferred_element_type=F32)
        cw = cw_ref[:, c0:c0 + wc]
        gc = (gate_scr[HALO - 1:HALO - 1 + tm, :] * cw[0:1, :] + gate_scr[HALO:HALO + tm, :] * cw[1:2, :]
              + gate_scr[HALO + 1:HALO + 1 + tm, :] * cw[2:3, :] + cb_ref[:, c0:c0 + wc])
        act = (jax.nn.gelu(gc) * val).astype(BF16)
        y = y + jnp.dot(act, wdn_ref[c0:c0 + wc, :], preferred_element_type=F32)
    if final_norm:
        y = _rms(y, gf_ref[...])
    o_ref[0] = y


def _ffn(x, gain, w_up, conv_w, conv_b, w_down, final_gain, final_norm, tm):
    b, l, d = x.shape
    dff = w_down.shape[0]
    r = tm // HALO
    nh = l // HALO
    return pl.pallas_call(
        functools.partial(_ffn_kernel, final_norm=final_norm),
        grid=(b, l // tm),
        in_specs=[pl.BlockSpec((1, HALO, d), lambda i, j: (i, jnp.maximum(j * r - 1, 0), 0)),
                  pl.BlockSpec((1, tm, d), lambda i, j: (i, j, 0)),
                  pl.BlockSpec((1, HALO, d), lambda i, j: (i, jnp.minimum((j + 1) * r, nh - 1), 0)),
                  _const_spec((1, d)), _const_spec((d, 2 * dff)), _const_spec((3, dff)),
                  _const_spec((1, dff)), _const_spec((dff, d)), _const_spec((1, d))],
        out_specs=pl.BlockSpec((1, tm, d), lambda i, j: (i, j, 0)),
        out_shape=jax.ShapeDtypeStruct((b, l, d), F32),
        scratch_shapes=[pltpu.VMEM((tm + 2 * HALO, d), BF16), pltpu.VMEM((tm + 2 * HALO, dff // 2), F32)],
        compiler_params=_cparams("parallel", "arbitrary"),
        name="ffn",
    )(x, x, x, gain, w_up, conv_w, conv_b, w_down, final_gain)


def _rope_tables(l):
    rows = l // GRID_W
    r = jnp.broadcast_to(jnp.arange(rows, dtype=F32)[:, None], (rows, GRID_W)).reshape(l)
    c = jnp.broadcast_to(jnp.arange(GRID_W, dtype=F32)[None, :], (rows, GRID_W)).reshape(l)
    freqs = ROPE_THETA ** (-jnp.arange(ROPE_PAIRS, dtype=F32) / ROPE_PAIRS)
    ang_r, ang_c = r[:, None] * freqs, c[:, None] * freqs
    ang = jnp.concatenate([ang_r, ang_r, ang_c, ang_c], axis=-1)
    cos, sin = jnp.cos(ang), jnp.sin(ang)
    first = (jnp.arange(HEAD_DIM) % (2 * ROPE_PAIRS)) < ROPE_PAIRS
    sin_up = jnp.where(first, -sin, 0.0)
    sin_dn = jnp.where(first, 0.0, sin)
    two = lambda a: jnp.concatenate([a, a], axis=1)
    return two(cos), two(sin_up), two(sin_dn)


def _prepare_layer(p, l, s5_chunk):
    bf = lambda a: a.astype(BF16)
    row = lambda a: a.astype(F32).reshape(1, -1)
    return dict(
        norm_mix=row(p['norm_mix'][l]), w_in=bf(p['w_in'][l]),
        q_gain=row(jnp.tile(p['q_norm'][l], N_HEADS)), k_gain=row(jnp.tile(p['k_norm'][l], N_KV_HEADS)),
        s5=_s5_operators(p['ssm_a_re'][l], p['ssm_a_im'][l], p['ssm_log_dt'][l], p['ssm_b_re'][l],
                         p['ssm_b_im'][l], p['ssm_c_re'][l], p['ssm_c_im'][l], s5_chunk),
        d_skip=row(p['ssm_d'][l]), w_glu=bf(p['ssm_glu'][l]),
        mem_norm=row(p['mem_norm'][l]), w_mem_kv=bf(p['w_mem_kv'][l]),
        p_attn=bf(p['p_attn'][l]), p_ssm=bf(p['p_ssm'][l]), p_cross=bf(p['p_cross'][l]), w_out=bf(p['w_out'][l]),
        norm_ffn=row(p['norm_ffn'][l]), w_up=bf(p['w_up'][l]), conv_w=p['conv_w'][l].astype(F32),
        conv_b=row(p['conv_b'][l]), w_down=bf(p['w_down'][l]),
    )


def _encode(x, mem, layers, final_gain, s5_chunk):
    b, l, d = x.shape
    tables = _rope_tables(l)
    tk = _pick(l, 512)
    tq = _pick(l, 256)
    tm = _pick(l, 512)
    for li, w in enumerate(layers):
        q, kt, v, u, qc, gates = _in_proj(x, w['norm_mix'], w['w_in'], tables, w['q_gain'], w['k_gain'], tk)
        attn = _attention(q, kt, v, tq)
        yscan = _s5_scan_branch(u, w['s5'], s5_chunk)
        mkv = _mem_kv(mem, w['mem_norm'], w['w_mem_kv'])
        x = _merge(x, attn, yscan, u, qc, gates, mkv, w['d_skip'], w['w_glu'], w['p_attn'], w['p_ssm'],
                   w['p_cross'], w['w_out'], tm)
        x = _ffn(x, w['norm_ffn'], w['w_up'], w['conv_w'], w['conv_b'], w['w_down'], final_gain,
                 li == len(layers) - 1, tm)
    return x


def kernel(x_prompt, x_sample, mem_prompt, mem_sample, norm_mix, w_in, q_norm, k_norm, ssm_a_re, ssm_a_im, ssm_log_dt, ssm_b_re, ssm_b_im, ssm_c_re, ssm_c_im, ssm_d, ssm_glu, mem_norm, w_mem_kv, p_attn, p_ssm, p_cross, w_out, norm_ffn, w_up, conv_w, conv_b, w_down, norm_final):
    p = dict(norm_mix=norm_mix, w_in=w_in, q_norm=q_norm, k_norm=k_norm, ssm_a_re=ssm_a_re, ssm_a_im=ssm_a_im,
             ssm_log_dt=ssm_log_dt, ssm_b_re=ssm_b_re, ssm_b_im=ssm_b_im, ssm_c_re=ssm_c_re, ssm_c_im=ssm_c_im,
             ssm_d=ssm_d, ssm_glu=ssm_glu, mem_norm=mem_norm, w_mem_kv=w_mem_kv, p_attn=p_attn, p_ssm=p_ssm,
             p_cross=p_cross, w_out=w_out, norm_ffn=norm_ffn, w_up=w_up, conv_w=conv_w, conv_b=conv_b,
             w_down=w_down)
    s5_chunk = 64
    layers = [_prepare_layer(p, l, s5_chunk) for l in range(norm_mix.shape[0])]
    final_gain = norm_final.astype(F32).reshape(1, -1)
    y_prompt = _encode(x_prompt, mem_prompt, layers, final_gain, s5_chunk)
    y_sample = _encode(x_sample, mem_sample, layers, final_gain, s5_chunk)
    return (y_prompt, y_sample)
```

```python
import functools
import math

import jax
import jax.numpy as jnp
from jax import lax
from jax.experimental import pallas as pl
from jax.experimental.pallas import tpu as pltpu

HEAD_DIM = 64
N_HEADS = 8
N_KV_HEADS = 2
KV_GROUP = N_HEADS // N_KV_HEADS
ATTN_W = N_HEADS * HEAD_DIM
KV_W = N_KV_HEADS * HEAD_DIM
SSM_GROUP_CH = 16
SSM_W = 256
SSM_GROUPS = SSM_W // SSM_GROUP_CH
SSM_STATE = 64
N_CROSS_HEADS = 4
CROSS_W = N_CROSS_HEADS * HEAD_DIM
N_BRANCHES = 3
GRID_W = 64
ROPE_THETA = 10000.0
ROPE_PAIRS = HEAD_DIM // 4
EPS = 1e-6

LOG2E = 1.4426950408889634
QK_SCALE = HEAD_DIM ** -0.5 * LOG2E
SAFE_SCORE = 64.0
LANES = 128
S5_CHUNK = LANES
HALO = 16
VMEM_LIMIT = 56 * 1024 * 1024

F32 = jnp.float32
BF16 = jnp.bfloat16


def _cparams(*sem):
    return pltpu.CompilerParams(dimension_semantics=sem, vmem_limit_bytes=VMEM_LIMIT)


def _const_spec(shape):
    nd = len(shape)
    return pl.BlockSpec(shape, lambda *_: (0,) * nd, pipeline_mode=pl.Buffered(1))


def _rms(x, gain):
    return x * lax.rsqrt(jnp.mean(x * x, axis=-1, keepdims=True) + EPS) * gain


def _pick(n, pref):
    t = min(n, pref)
    while n % t:
        t //= 2
    return t


def _head_norm_rope(x, gain, cos, sin_up, sin_dn, out_scale):
    w = x.shape[1]
    reps = w // LANES
    tile = lambda a: a if reps == 1 else jnp.concatenate([a] * reps, axis=1)
    xg = x * gain
    y = (xg * tile(cos)
         + pltpu.roll(xg, w - ROPE_PAIRS, 1) * tile(sin_up)
         + pltpu.roll(xg, ROPE_PAIRS, 1) * tile(sin_dn))
    lane = lax.broadcasted_iota(jnp.int32, (1, LANES), 1)
    low = lane < HEAD_DIM
    outs = []
    for p in range(reps):
        xp = x[:, p * LANES:(p + 1) * LANES]
        sq = xp * xp
        ss_all = jnp.sum(sq, axis=-1, keepdims=True)
        ss_lo = jnp.sum(jnp.where(low, sq, 0.0), axis=-1, keepdims=True)
        r_lo = lax.rsqrt(ss_lo * (1.0 / HEAD_DIM) + EPS) * out_scale
        r_hi = lax.rsqrt((ss_all - ss_lo) * (1.0 / HEAD_DIM) + EPS) * out_scale
        outs.append(y[:, p * LANES:(p + 1) * LANES] * jnp.where(low, r_lo, r_hi))
    return outs[0] if reps == 1 else jnp.concatenate(outs, axis=1)


def _in_proj_kernel(x_ref, g_ref, w_ref, cos_ref, sup_ref, sdn_ref, qg_ref, kg_ref,
                    q_ref, kt_ref, va_ref, u_ref, ut_ref, qc_ref, gate_ref):
    h = _rms(x_ref[0], g_ref[...]).astype(BF16)

    def seg(a, b):
        return jnp.dot(h, w_ref[:, a:b], preferred_element_type=F32)

    cos, sup, sdn = cos_ref[...], sup_ref[...], sdn_ref[...]
    o = 0
    q = seg(o, o + ATTN_W)
    q_ref[0] = _head_norm_rope(q, qg_ref[...], cos, sup, sdn, QK_SCALE).astype(BF16)
    o += ATTN_W
    k = _head_norm_rope(seg(o, o + KV_W), kg_ref[...], cos, sup, sdn, 1.0)
    kt_ref[0, 0] = k.T.astype(BF16)
    o += KV_W
    v = seg(o, o + KV_W)
    first = lax.broadcasted_iota(jnp.int32, (1, LANES), 1) < HEAD_DIM
    va_ref[0, 0] = jnp.where(first, v, 1.0).astype(BF16)
    va_ref[0, 1] = jnp.where(first, pltpu.roll(v, HEAD_DIM, 1), 1.0).astype(BF16)
    o += KV_W
    u = seg(o, o + SSM_W)
    u_ref[0] = u.astype(BF16)
    ut_ref[0] = u.T.astype(BF16)
    o += SSM_W
    qc_ref[0] = (seg(o, o + CROSS_W) * HEAD_DIM ** -0.5).astype(BF16)
    o += CROSS_W
    d = x_ref.shape[2]
    for j in range(N_BRANCHES):
        gate_ref[0, :, j * d:(j + 1) * d] = jax.nn.sigmoid(seg(o + j * d, o + (j + 1) * d)).astype(BF16)


def _in_proj(x, gain, w, tables, q_gain, k_gain, tk):
    b, l, d = x.shape
    n_in = w.shape[1]
    nj = l // tk
    row = lambda width: pl.BlockSpec((1, tk, width), lambda i, j: (i, j, 0))
    tab = pl.BlockSpec((tk, LANES), lambda i, j: (j, 0))
    out_shape = (
        jax.ShapeDtypeStruct((b, l, ATTN_W), BF16),
        jax.ShapeDtypeStruct((b, nj, KV_W, tk), BF16),
        jax.ShapeDtypeStruct((b, N_KV_HEADS, l, LANES), BF16),
        jax.ShapeDtypeStruct((b, l, SSM_W), BF16),
        jax.ShapeDtypeStruct((b, SSM_W, l), BF16),
        jax.ShapeDtypeStruct((b, l, CROSS_W), BF16),
        jax.ShapeDtypeStruct((b, l, N_BRANCHES * d), BF16),
    )
    return pl.pallas_call(
        _in_proj_kernel,
        grid=(b, nj),
        in_specs=[row(d), _const_spec((1, d)), _const_spec((d, n_in)), tab, tab, tab,
                  _const_spec((1, ATTN_W)), _const_spec((1, KV_W))],
        out_specs=(row(ATTN_W), pl.BlockSpec((1, 1, KV_W, tk), lambda i, j: (i, j, 0, 0)),
                   pl.BlockSpec((1, N_KV_HEADS, tk, LANES), lambda i, j: (i, 0, j, 0)),
                   row(SSM_W), pl.BlockSpec((1, SSM_W, tk), lambda i, j: (i, 0, j)),
                   row(CROSS_W), row(N_BRANCHES * d)),
        out_shape=out_shape,
        compiler_params=_cparams("parallel", "parallel"),
        name="in_proj",
    )(x, gain, w, *tables, q_gain, k_gain)


def _head_lanes(pair, h):
    return pair[:, (h % 2) * HEAD_DIM:(h % 2 + 1) * HEAD_DIM]


def _attn_kernel(q_ref, kt_ref, va_ref, o_ref, qs_ref, acc_ref, m_ref, *, bounded):
    tq = q_ref.shape[1]
    nc, tk = kt_ref.shape[1], kt_ref.shape[3]
    for h in range(N_HEADS):
        g, r = divmod(h, KV_GROUP)
        qs_ref[g, r * tq:(r + 1) * tq, :] = _head_lanes(q_ref[0, :, (h // 2) * LANES:(h // 2 + 1) * LANES], h)
    acc_ref[...] = jnp.zeros(acc_ref.shape, F32)
    if not bounded:
        m_ref[...] = jnp.full(m_ref.shape, -jnp.inf, F32)

    def body(c, carry):
        rows = pl.ds(pl.multiple_of(c * tk, tk), tk)
        for g in range(N_KV_HEADS):
            kt = kt_ref[0, c, g * HEAD_DIM:(g + 1) * HEAD_DIM, :]
            s = jnp.dot(qs_ref[g], kt, preferred_element_type=F32)
            if bounded:
                p = jnp.exp2(s).astype(BF16)
                acc_ref[g] += jnp.dot(p, va_ref[0, g, rows, :], preferred_element_type=F32)
            else:
                m_old = m_ref[g]
                m_new = jnp.maximum(m_old, jnp.max(s, axis=-1, keepdims=True))
                p = jnp.exp2(s - m_new).astype(BF16)
                acc_ref[g] = (jnp.exp2(m_old - m_new) * acc_ref[g]
                              + jnp.dot(p, va_ref[0, g, rows, :], preferred_element_type=F32))
                m_ref[g] = m_new
        return carry

    lax.fori_loop(0, nc, body, 0)
    outs = []
    for h in range(N_HEADS):
        g, r = divmod(h, KV_GROUP)
        acc = acc_ref[g, r * tq:(r + 1) * tq, :]
        outs.append(acc[:, :HEAD_DIM] / acc[:, HEAD_DIM:])
    o_ref[0] = jnp.concatenate(outs, axis=1).astype(BF16)


def _attention(q, kt, va, tq, score_bound):
    b, l, _ = q.shape
    nc, tk = kt.shape[1], kt.shape[3]

    def call(bounded):
        return pl.pallas_call(
            functools.partial(_attn_kernel, bounded=bounded),
            grid=(b, l // tq),
            in_specs=[pl.BlockSpec((1, tq, ATTN_W), lambda i, j: (i, j, 0)),
                      pl.BlockSpec((1, nc, KV_W, tk), lambda i, j: (i, 0, 0, 0)),
                      pl.BlockSpec((1, N_KV_HEADS, l, LANES), lambda i, j: (i, 0, 0, 0))],
            out_specs=pl.BlockSpec((1, tq, ATTN_W), lambda i, j: (i, j, 0)),
            out_shape=jax.ShapeDtypeStruct((b, l, ATTN_W), BF16),
            scratch_shapes=[pltpu.VMEM((N_KV_HEADS, KV_GROUP * tq, HEAD_DIM), BF16),
                            pltpu.VMEM((N_KV_HEADS, KV_GROUP * tq, LANES), F32),
                            pltpu.VMEM((N_KV_HEADS, KV_GROUP * tq, 1), F32)],
            compiler_params=_cparams("parallel", "parallel"),
            name="attention" if bounded else "attention_running_max",
        )

    return lax.cond(score_bound <= SAFE_SCORE, call(True), call(False), q, kt, va)


def _complex_powers(lr, li, n):
    pr, pi = jnp.ones_like(lr)[None], jnp.zeros_like(li)[None]
    cr, ci = lr, li
    while pr.shape[0] < n:
        pr, pi = (jnp.concatenate([pr, pr * cr - pi * ci]), jnp.concatenate([pi, pr * ci + pi * cr]))
        cr, ci = cr * cr - ci * ci, 2.0 * cr * ci
    return pr[:n], pi[:n]


def _s5_operators(a_re, a_im, log_dt, b_re, b_im, c_re, c_im, t):
    hi = lax.Precision.HIGHEST
    a_re, a_im, log_dt = a_re.astype(F32), a_im.astype(F32), log_dt.astype(F32)
    b_re, b_im, c_re, c_im = (z.astype(F32) for z in (b_re, b_im, c_re, c_im))
    dt = jnp.exp(log_dt)[..., None]
    mag = jnp.exp(a_re * dt)
    lam_re, lam_im = mag * jnp.cos(a_im * dt), mag * jnp.sin(a_im * dt)
    num_re = lam_re - 1.0
    den = a_re * a_re + a_im * a_im
    coef_re = (num_re * a_re + lam_im * a_im) / den
    coef_im = (lam_im * a_re - num_re * a_im) / den
    pw_re, pw_im = _complex_powers(lam_re, lam_im, t + 1)
    e_re = pw_re * coef_re - pw_im * coef_im
    e_im = pw_re * coef_im + pw_im * coef_re
    g, p, hch = b_re.shape

    cb_re = c_re[:, :, :, None] * b_re[:, None, :, :] - c_im[:, :, :, None] * b_im[:, None, :, :]
    cb_im = c_re[:, :, :, None] * b_im[:, None, :, :] + c_im[:, :, :, None] * b_re[:, None, :, :]
    kern = (jnp.einsum('tdgp,ghpk->dgthk', e_re[:t], cb_re, precision=hi)
            - jnp.einsum('tdgp,ghpk->dgthk', e_im[:t], cb_im, precision=hi))
    k_f, k_b = kern[0], kern[1]
    k_cat = jnp.concatenate([k_b[:, :0:-1], (k_f[:, :1] + k_b[:, :1]), k_f[:, 1:]], axis=1)
    k_cat = k_cat.transpose(0, 3, 2, 1)
    toep = jnp.stack([k_cat[..., t - 1 - s:2 * t - 1 - s] for s in range(t)], axis=2)
    toep = toep.reshape(g, hch * t, hch * t)

    def times_b(er, ei):
        re = er[:, :, :, None] * b_re[None] - ei[:, :, :, None] * b_im[None]
        im = er[:, :, :, None] * b_im[None] + ei[:, :, :, None] * b_re[None]
        f = lambda z: z.transpose(1, 3, 0, 2).reshape(g, hch * t, p)
        return f(re), f(im)
    sf_re, sf_im = times_b(e_re[:t, 0][::-1], e_im[:t, 0][::-1])
    sb_re, sb_im = times_b(e_re[:t, 1], e_im[:t, 1])
    w_s = jnp.concatenate([sf_re, sf_im, sb_re, sb_im], axis=2)

    def c_times(pr, pi):
        re = c_re[None] * pr[:, :, None, :] - c_im[None] * pi[:, :, None, :]
        im = c_re[None] * pi[:, :, None, :] + c_im[None] * pr[:, :, None, :]
        f = lambda z: z.transpose(1, 3, 2, 0).reshape(g, p, hch * t)
        return f(re), f(-im)
    of_re, of_im = c_times(pw_re[1:t + 1, 0], pw_im[1:t + 1, 0])
    ob_re, ob_im = c_times(pw_re[1:t + 1, 1][::-1], pw_im[1:t + 1, 1][::-1])
    w_o = jnp.concatenate([of_re, of_im, ob_re, ob_im], axis=1)

    lt_re, lt_im = pw_re[t], pw_im[t]
    a1 = jnp.concatenate([lt_re[0], lt_re[0], lt_re[1], lt_re[1]], axis=-1)[:, None, :]
    a2 = jnp.concatenate([-lt_im[0], lt_im[0], -lt_im[1], lt_im[1]], axis=-1)[:, None, :]
    return toep.astype(BF16), w_s.astype(BF16), w_o.astype(BF16), a1, a2


def _s5_chunk_rows(u_ref):
    b, hch, c, t = u_ref.shape
    return jnp.concatenate([u_ref[:, h].reshape(b * c, t) for h in range(hch)], axis=1)


def _s5_state_kernel(u_ref, ws_ref, s_ref):
    s_ref[0] = jnp.dot(_s5_chunk_rows(u_ref), ws_ref[0], preferred_element_type=F32)


def _s5_state(ut, w_s):
    b, ch, c, t = ut.shape
    g, k, n = w_s.shape
    return pl.pallas_call(
        _s5_state_kernel,
        grid=(g,),
        in_specs=[pl.BlockSpec((b, ch // g, c, t), lambda i: (0, i, 0, 0)),
                  pl.BlockSpec((1, k, n), lambda i: (i, 0, 0))],
        out_specs=pl.BlockSpec((1, b * c, n), lambda i: (i, 0, 0)),
        out_shape=jax.ShapeDtypeStruct((g, b * c, n), F32),
        compiler_params=_cparams("parallel"),
        name="s5_state",
    )(ut, w_s)


def _s5_scan_kernel(s_ref, a1_ref, a2_ref, h_ref, *, nb):
    gb, rows, _ = s_ref.shape
    nchunk = rows // nb
    half = 2 * SSM_STATE
    a1, a2 = a1_ref[...], a2_ref[...]
    a1f, a1b, a2f, a2b = a1[:, :, :half], a1[:, :, half:], a2[:, :, :half], a2[:, :, half:]

    def swap(z):
        return pltpu.roll(z.reshape(gb * nb, half), SSM_STATE, 1).reshape(gb, nb, half)

    def body(j, carry):
        hf, hb = carry
        rf = pl.ds(pl.multiple_of(j * nb, nb), nb)
        rb = pl.ds(pl.multiple_of((nchunk - 1 - j) * nb, nb), nb)
        h_ref[:, rf, :half] = hf
        h_ref[:, rb, half:] = hb
        hf = a1f * hf + a2f * swap(hf) + s_ref[:, rf, :half]
        hb = a1b * hb + a2b * swap(hb) + s_ref[:, rb, half:]
        return hf, hb

    zero = jnp.zeros((gb, nb, half), F32)
    lax.fori_loop(0, nchunk, body, (zero, zero))


def _s5_scan(s, a1, a2, nb, gb):
    g, r, n = s.shape
    blk = pl.BlockSpec((gb, r, n), lambda i: (i, 0, 0))
    par = pl.BlockSpec((gb, 1, n), lambda i: (i, 0, 0))
    return pl.pallas_call(
        functools.partial(_s5_scan_kernel, nb=nb),
        grid=(g // gb,),
        in_specs=[blk, par, par],
        out_specs=blk,
        out_shape=jax.ShapeDtypeStruct((g, r, n), F32),
        compiler_params=_cparams("parallel"),
        name="s5_scan",
    )(s, a1, a2)


def _s5_out_kernel(u_ref, h_ref, toep_ref, wo_ref, y_ref):
    b, hch, c, t = u_ref.shape
    y = jnp.dot(_s5_chunk_rows(u_ref), toep_ref[0], preferred_element_type=F32)
    y = y + jnp.dot(h_ref[0].astype(BF16), wo_ref[0], preferred_element_type=F32)
    for h in range(hch):
        y_ref[:, h] = y[:, h * t:(h + 1) * t].reshape(b, c, t)


def _s5_out(ut, hin, toep, w_o):
    b, ch, c, t = ut.shape
    g, n, k = w_o.shape
    blk = pl.BlockSpec((b, ch // g, c, t), lambda i: (0, i, 0, 0))
    return pl.pallas_call(
        _s5_out_kernel,
        grid=(g,),
        in_specs=[blk, pl.BlockSpec((1, b * c, n), lambda i: (i, 0, 0)),
                  pl.BlockSpec((1, k, k), lambda i: (i, 0, 0)),
                  pl.BlockSpec((1, n, k), lambda i: (i, 0, 0))],
        out_specs=blk,
        out_shape=jax.ShapeDtypeStruct((b, ch, c, t), F32),
        compiler_params=_cparams("parallel"),
        name="s5_out",
    )(ut, hin, toep, w_o)


def _s5_scan_branch(ut, ops):
    toep, w_s, w_o, a1, a2 = ops
    b, ch, l = ut.shape
    t = toep.shape[1] // SSM_GROUP_CH
    c = l // t
    g, n = w_s.shape[0], w_s.shape[2]
    ut = ut.reshape(b, ch, c, t)
    s = _s5_state(ut, w_s)
    s = s.reshape(g, b, c, n).transpose(0, 2, 1, 3).reshape(g, c * b, n)
    hin = _s5_scan(s, a1, a2, b, 4)
    hin = hin.reshape(g, c, b, n).transpose(0, 2, 1, 3).reshape(g, b * c, n)
    return _s5_out(ut, hin, toep, w_o).reshape(b, ch, l)


def _mem_kv_kernel(m_ref, g_ref, w_ref, o_ref):
    h = _rms(m_ref[0], g_ref[...]).astype(BF16)
    o_ref[0] = jnp.dot(h, w_ref[...], preferred_element_type=F32).astype(BF16)


def _mem_kv(mem, gain, w):
    b, m, d = mem.shape
    n = w.shape[1]
    return pl.pallas_call(
        _mem_kv_kernel,
        grid=(b,),
        in_specs=[pl.BlockSpec((1, m, d), lambda i: (i, 0, 0)), _const_spec((1, d)), _const_spec((d, n))],
        out_specs=pl.BlockSpec((1, m, n), lambda i: (i, 0, 0)),
        out_shape=jax.ShapeDtypeStruct((b, m, n), BF16),
        compiler_params=_cparams("parallel"),
        name="mem_kv",
    )(mem, gain, w)


def _merge_kernel(x_ref, a_ref, y_ref, u_ref, qc_ref, gate_ref, mkv_ref, dskip_ref, wglu_ref,
                  pa_ref, ps_ref, pc_ref, wout_ref, o_ref):
    d = x_ref.shape[2]
    mv = mkv_ref[0, :, CROSS_W:]
    heads = []
    for h in range(N_CROSS_HEADS):
        sl = slice(h * HEAD_DIM, (h + 1) * HEAD_DIM)
        pair = slice((h // 2) * LANES, (h // 2 + 1) * LANES)
        s = lax.dot_general(_head_lanes(qc_ref[0, :, pair], h), _head_lanes(mkv_ref[0, :, pair], h),
                            (((1,), (1,)), ((), ())), preferred_element_type=F32)
        p = jnp.exp(s - jnp.max(s, axis=-1, keepdims=True))
        den = jnp.sum(p, axis=-1, keepdims=True)
        o = jnp.dot(p.astype(BF16), mv, preferred_element_type=F32)
        heads.append(o[:, sl] / den)
    cross = jnp.concatenate(heads, axis=1).astype(BF16)
    y = y_ref[0].T + dskip_ref[...] * u_ref[0].astype(F32)
    y = jax.nn.gelu(y)
    y = y * jax.nn.sigmoid(jnp.dot(y.astype(BF16), wglu_ref[...], preferred_element_type=F32))
    m = gate_ref[0, :, 0:d].astype(F32) * jnp.dot(a_ref[0], pa_ref[...], preferred_element_type=F32)
    m = m + gate_ref[0, :, d:2 * d].astype(F32) * jnp.dot(y.astype(BF16), ps_ref[...], preferred_element_type=F32)
    m = m + gate_ref[0, :, 2 * d:3 * d].astype(F32) * jnp.dot(cross, pc_ref[...], preferred_element_type=F32)
    o_ref[0] = x_ref[0] + jnp.dot(m.astype(BF16), wout_ref[...], preferred_element_type=F32)


def _merge(x, attn, yscan, u, qc, gates, mkv, d_skip, w_glu, p_attn, p_ssm, p_cross, w_out, tl):
    b, l, d = x.shape
    row = lambda width: pl.BlockSpec((1, tl, width), lambda i, j: (i, j, 0))
    n_mem = mkv.shape[1]
    return pl.pallas_call(
        _merge_kernel,
        grid=(b, l // tl),
        in_specs=[row(d), row(ATTN_W), pl.BlockSpec((1, SSM_W, tl), lambda i, j: (i, 0, j)),
                  row(SSM_W), row(CROSS_W), row(N_BRANCHES * d),
                  pl.BlockSpec((1, n_mem, 2 * CROSS_W), lambda i, j: (i, 0, 0)),
                  _const_spec((1, SSM_W)), _const_spec((SSM_W, SSM_W)),
                  _const_spec((ATTN_W, d)), _const_spec((SSM_W, d)), _const_spec((CROSS_W, d)),
                  _const_spec((d, d))],
        out_specs=row(d),
        out_shape=jax.ShapeDtypeStruct((b, l, d), F32),
        compiler_params=_cparams("parallel", "parallel"),
        name="merge",
    )(x, attn, yscan, u, qc, gates, mkv, d_skip, w_glu, p_attn, p_ssm, p_cross, w_out)


def _ffn_kernel(xp_ref, x_ref, xn_ref, g_ref, wup_ref, cw_ref, cb_ref, wdn_ref, gf_ref, o_ref,
                h_scr, gate_scr, *, final_norm):
    j = pl.program_id(1)
    nj = pl.num_programs(1)
    tm = x_ref.shape[1]
    dff = wdn_ref.shape[0]
    gain = g_ref[...]
    x = x_ref[0]
    h_scr[0:HALO, :] = _rms(xp_ref[0], gain).astype(BF16)
    h_scr[HALO:HALO + tm, :] = _rms(x, gain).astype(BF16)
    h_scr[HALO + tm:, :] = _rms(xn_ref[0], gain).astype(BF16)
    wc = gate_scr.shape[1]
    y = x
    for c0 in range(0, dff, wc):
        gate_scr[...] = jnp.dot(h_scr[...], wup_ref[:, c0:c0 + wc], preferred_element_type=F32)
        @pl.when(j == 0)
        def _():
            gate_scr[0:HALO, :] = jnp.zeros((HALO, wc), F32)

        @pl.when(j == nj - 1)
        def _():
            gate_scr[HALO + tm:, :] = jnp.zeros((HALO, wc), F32)

        val = jnp.dot(h_scr[HALO:HALO + tm, :], wup_ref[:, dff + c0:dff + c0 + wc], preferred_element_type=F32)
        cw = cw_ref[:, c0:c0 + wc]
        gc = (gate_scr[HALO - 1:HALO - 1 + tm, :] * cw[0:1, :] + gate_scr[HALO:HALO + tm, :] * cw[1:2, :]
              + gate_scr[HALO + 1:HALO + 1 + tm, :] * cw[2:3, :] + cb_ref[:, c0:c0 + wc])
        act = (jax.nn.gelu(gc) * val).astype(BF16)
        y = y + jnp.dot(act, wdn_ref[c0:c0 + wc, :], preferred_element_type=F32)
    if final_norm:
        y = _rms(y, gf_ref[...])
    o_ref[0] = y


def _ffn(x, gain, w_up, conv_w, conv_b, w_down, final_gain, final_norm, tm):
    b, l, d = x.shape
    dff = w_down.shape[0]
    r = tm // HALO
    nh = l // HALO
    return pl.pallas_call(
        functools.partial(_ffn_kernel, final_norm=final_norm),
        grid=(b, l // tm),
        in_specs=[pl.BlockSpec((1, HALO, d), lambda i, j: (i, jnp.maximum(j * r - 1, 0), 0)),
                  pl.BlockSpec((1, tm, d), lambda i, j: (i, j, 0)),
                  pl.BlockSpec((1, HALO, d), lambda i, j: (i, jnp.minimum((j + 1) * r, nh - 1), 0)),
                  _const_spec((1, d)), _const_spec((d, 2 * dff)), _const_spec((3, dff)),
                  _const_spec((1, dff)), _const_spec((dff, d)), _const_spec((1, d))],
        out_specs=pl.BlockSpec((1, tm, d), lambda i, j: (i, j, 0)),
        out_shape=jax.ShapeDtypeStruct((b, l, d), F32),
        scratch_shapes=[pltpu.VMEM((tm + 2 * HALO, d), BF16), pltpu.VMEM((tm + 2 * HALO, dff // 2), F32)],
        compiler_params=_cparams("parallel", "arbitrary"),
        name="ffn",
    )(x, x, x, gain, w_up, conv_w, conv_b, w_down, final_gain)


def _rope_tables(l):
    rows = l // GRID_W
    r = jnp.broadcast_to(jnp.arange(rows, dtype=F32)[:, None], (rows, GRID_W)).reshape(l)
    c = jnp.broadcast_to(jnp.arange(GRID_W, dtype=F32)[None, :], (rows, GRID_W)).reshape(l)
    freqs = ROPE_THETA ** (-jnp.arange(ROPE_PAIRS, dtype=F32) / ROPE_PAIRS)
    ang_r, ang_c = r[:, None] * freqs, c[:, None] * freqs
    ang = jnp.concatenate([ang_r, ang_r, ang_c, ang_c], axis=-1)
    cos, sin = jnp.cos(ang), jnp.sin(ang)
    first = (jnp.arange(HEAD_DIM) % (2 * ROPE_PAIRS)) < ROPE_PAIRS
    sin_up = jnp.where(first, -sin, 0.0)
    sin_dn = jnp.where(first, 0.0, sin)
    two = lambda a: jnp.concatenate([a, a], axis=1)
    return two(cos), two(sin_up), two(sin_dn)


def _prepare_layer(p, l):
    bf = lambda a: a.astype(BF16)
    row = lambda a: a.astype(F32).reshape(1, -1)
    max_abs = lambda a: jnp.max(jnp.abs(a.astype(F32)))
    score_bound = 1.02 * QK_SCALE * HEAD_DIM * max_abs(p['q_norm'][l]) * max_abs(p['k_norm'][l])
    return dict(
        norm_mix=row(p['norm_mix'][l]), w_in=bf(p['w_in'][l]),
        q_gain=row(jnp.tile(p['q_norm'][l], N_HEADS)), k_gain=row(jnp.tile(p['k_norm'][l], N_KV_HEADS)),
        score_bound=score_bound,
        s5=_s5_operators(p['ssm_a_re'][l], p['ssm_a_im'][l], p['ssm_log_dt'][l], p['ssm_b_re'][l],
                         p['ssm_b_im'][l], p['ssm_c_re'][l], p['ssm_c_im'][l], S5_CHUNK),
        d_skip=row(p['ssm_d'][l]), w_glu=bf(p['ssm_glu'][l]),
        mem_norm=row(p['mem_norm'][l]), w_mem_kv=bf(p['w_mem_kv'][l]),
        p_attn=bf(p['p_attn'][l]), p_ssm=bf(p['p_ssm'][l]), p_cross=bf(p['p_cross'][l]), w_out=bf(p['w_out'][l]),
        norm_ffn=row(p['norm_ffn'][l]), w_up=bf(p['w_up'][l]), conv_w=p['conv_w'][l].astype(F32),
        conv_b=row(p['conv_b'][l]), w_down=bf(p['w_down'][l]),
    )


def _encode(x, mem, layers, final_gain):
    b, l, d = x.shape
    tables = _rope_tables(l)
    tk = _pick(l, 512)
    tq = _pick(l, 256)
    tm = _pick(l, 512)
    for li, w in enumerate(layers):
        q, kt, va, u, ut, qc, gates = _in_proj(x, w['norm_mix'], w['w_in'], tables, w['q_gain'], w['k_gain'], tk)
        attn = _attention(q, kt, va, tq, w['score_bound'])
        yscan = _s5_scan_branch(ut, w['s5'])
        mkv = _mem_kv(mem, w['mem_norm'], w['w_mem_kv'])
        x = _merge(x, attn, yscan, u, qc, gates, mkv, w['d_skip'], w['w_glu'], w['p_attn'], w['p_ssm'],
                   w['p_cross'], w['w_out'], tm)
        x = _ffn(x, w['norm_ffn'], w['w_up'], w['conv_w'], w['conv_b'], w['w_down'], final_gain,
                 li == len(layers) - 1, tm)
    return x


def kernel(x_prompt, x_sample, mem_prompt, mem_sample, norm_mix, w_in, q_norm, k_norm, ssm_a_re, ssm_a_im, ssm_log_dt, ssm_b_re, ssm_b_im, ssm_c_re, ssm_c_im, ssm_d, ssm_glu, mem_norm, w_mem_kv, p_attn, p_ssm, p_cross, w_out, norm_ffn, w_up, conv_w, conv_b, w_down, norm_final):
    p = dict(norm_mix=norm_mix, w_in=w_in, q_norm=q_norm, k_norm=k_norm, ssm_a_re=ssm_a_re, ssm_a_im=ssm_a_im,
             ssm_log_dt=ssm_log_dt, ssm_b_re=ssm_b_re, ssm_b_im=ssm_b_im, ssm_c_re=ssm_c_re, ssm_c_im=ssm_c_im,
             ssm_d=ssm_d, ssm_glu=ssm_glu, mem_norm=mem_norm, w_mem_kv=w_mem_kv, p_attn=p_attn, p_ssm=p_ssm,
             p_cross=p_cross, w_out=w_out, norm_ffn=norm_ffn, w_up=w_up, conv_w=conv_w, conv_b=conv_b,
             w_down=w_down)
    layers = [_prepare_layer(p, l) for l in range(norm_mix.shape[0])]
    final_gain = norm_final.astype(F32).reshape(1, -1)
    y_prompt = _encode(x_prompt, mem_prompt, layers, final_gain)
    y_sample = _encode(x_sample, mem_sample, layers, final_gain)
    return (y_prompt, y_sample)
```

```python
import functools
import math

import jax
import jax.numpy as jnp
from jax import lax
from jax.experimental import pallas as pl
from jax.experimental.pallas import tpu as pltpu

HEAD_DIM = 64
N_HEADS = 8
N_KV_HEADS = 2
KV_GROUP = N_HEADS // N_KV_HEADS
ATTN_W = N_HEADS * HEAD_DIM
KV_W = N_KV_HEADS * HEAD_DIM
SSM_GROUP_CH = 16
SSM_W = 256
SSM_GROUPS = SSM_W // SSM_GROUP_CH
SSM_STATE = 64
N_CROSS_HEADS = 4
CROSS_W = N_CROSS_HEADS * HEAD_DIM
N_BRANCHES = 3
GRID_W = 64
ROPE_THETA = 10000.0
ROPE_PAIRS = HEAD_DIM // 4
EPS = 1e-6

LOG2E = 1.4426950408889634
QK_SCALE = HEAD_DIM ** -0.5 * LOG2E
SAFE_SCORE = 64.0
LANES = 128
S5_CHUNK = LANES
HALO = 16
VMEM_LIMIT = 56 * 1024 * 1024

F32 = jnp.float32
BF16 = jnp.bfloat16


def _cparams(*sem):
    return pltpu.CompilerParams(dimension_semantics=sem, vmem_limit_bytes=VMEM_LIMIT)


def _const_spec(shape):
    nd = len(shape)
    return pl.BlockSpec(shape, lambda *_: (0,) * nd, pipeline_mode=pl.Buffered(1))


def _rms(x, gain):
    return x * lax.rsqrt(jnp.mean(x * x, axis=-1, keepdims=True) + EPS) * gain


def _pick(n, pref):
    t = min(n, pref)
    while n % t:
        t //= 2
    return t


def _head_norm_rope(x, gain, cos, sin_up, sin_dn, out_scale):
    w = x.shape[1]
    reps = w // LANES
    tile = lambda a: a if reps == 1 else jnp.concatenate([a] * reps, axis=1)
    xg = x * gain
    y = (xg * tile(cos)
         + pltpu.roll(xg, w - ROPE_PAIRS, 1) * tile(sin_up)
         + pltpu.roll(xg, ROPE_PAIRS, 1) * tile(sin_dn))
    lane = lax.broadcasted_iota(jnp.int32, (1, LANES), 1)
    low = lane < HEAD_DIM
    outs = []
    for p in range(reps):
        xp = x[:, p * LANES:(p + 1) * LANES]
        sq = xp * xp
        ss_all = jnp.sum(sq, axis=-1, keepdims=True)
        ss_lo = jnp.sum(jnp.where(low, sq, 0.0), axis=-1, keepdims=True)
        r_lo = lax.rsqrt(ss_lo * (1.0 / HEAD_DIM) + EPS) * out_scale
        r_hi = lax.rsqrt((ss_all - ss_lo) * (1.0 / HEAD_DIM) + EPS) * out_scale
        outs.append(y[:, p * LANES:(p + 1) * LANES] * jnp.where(low, r_lo, r_hi))
    return outs[0] if reps == 1 else jnp.concatenate(outs, axis=1)


def _in_proj_kernel(x_ref, g_ref, w_ref, cos_ref, sup_ref, sdn_ref, qg_ref, kg_ref,
                    qt_ref, k_ref, vt_ref, u_ref, ut_ref, qc_ref, gate_ref):
    h = _rms(x_ref[0], g_ref[...]).astype(BF16)

    def seg(a, b):
        return jnp.dot(h, w_ref[:, a:b], preferred_element_type=F32)

    cos, sup, sdn = cos_ref[...], sup_ref[...], sdn_ref[...]
    o = 0
    q = seg(o, o + ATTN_W)
    qt_ref[0] = _head_norm_rope(q, qg_ref[...], cos, sup, sdn, QK_SCALE).T.astype(BF16)
    o += ATTN_W
    kv = seg(o, o + 2 * KV_W)
    k_ref[0] = _head_norm_rope(kv[:, :KV_W], kg_ref[...], cos, sup, sdn, 1.0).astype(BF16)
    vt = kv[:, KV_W:].T.astype(BF16)
    ones = jnp.ones((HEAD_DIM, vt.shape[1]), BF16)
    for g in range(N_KV_HEADS):
        vt_ref[0, 0, g, :HEAD_DIM, :] = vt[g * HEAD_DIM:(g + 1) * HEAD_DIM, :]
        vt_ref[0, 0, g, HEAD_DIM:, :] = ones
    o += 2 * KV_W
    u = seg(o, o + SSM_W)
    u_ref[0] = u.astype(BF16)
    ut_ref[0] = u.T.astype(BF16)
    o += SSM_W
    qc_ref[0] = (seg(o, o + CROSS_W) * HEAD_DIM ** -0.5).astype(BF16)
    o += CROSS_W
    d = x_ref.shape[2]
    for j in range(N_BRANCHES):
        gate_ref[0, :, j * d:(j + 1) * d] = jax.nn.sigmoid(seg(o + j * d, o + (j + 1) * d)).astype(BF16)


def _in_proj(x, gain, w, tables, q_gain, k_gain, tk):
    b, l, d = x.shape
    n_in = w.shape[1]
    nj = l // tk
    row = lambda width: pl.BlockSpec((1, tk, width), lambda i, j: (i, j, 0))
    tab = pl.BlockSpec((tk, LANES), lambda i, j: (j, 0))
    out_shape = (
        jax.ShapeDtypeStruct((b, ATTN_W, l), BF16),
        jax.ShapeDtypeStruct((b, l, KV_W), BF16),
        jax.ShapeDtypeStruct((b, nj, N_KV_HEADS, 2 * HEAD_DIM, tk), BF16),
        jax.ShapeDtypeStruct((b, l, SSM_W), BF16),
        jax.ShapeDtypeStruct((b, SSM_W, l), BF16),
        jax.ShapeDtypeStruct((b, l, CROSS_W), BF16),
        jax.ShapeDtypeStruct((b, l, N_BRANCHES * d), BF16),
    )
    return pl.pallas_call(
        _in_proj_kernel,
        grid=(b, nj),
        in_specs=[row(d), _const_spec((1, d)), _const_spec((d, n_in)), tab, tab, tab,
                  _const_spec((1, ATTN_W)), _const_spec((1, KV_W))],
        out_specs=(pl.BlockSpec((1, ATTN_W, tk), lambda i, j: (i, 0, j)), row(KV_W),
                   pl.BlockSpec((1, 1, N_KV_HEADS, 2 * HEAD_DIM, tk), lambda i, j: (i, j, 0, 0, 0)),
                   row(SSM_W), pl.BlockSpec((1, SSM_W, tk), lambda i, j: (i, 0, j)),
                   row(CROSS_W), row(N_BRANCHES * d)),
        out_shape=out_shape,
        compiler_params=_cparams("parallel", "parallel"),
        name="in_proj",
    )(x, gain, w, *tables, q_gain, k_gain)


def _head_lanes(pair, h):
    return pair[:, (h % 2) * HEAD_DIM:(h % 2 + 1) * HEAD_DIM]


def _attn_kernel(qt_ref, k_ref, vt_ref, o_ref, qs_ref, acc_ref, m_ref, *, bounded):
    tq = qt_ref.shape[2]
    nc, tk = vt_ref.shape[1], vt_ref.shape[4]
    qs_ref[...] = jnp.zeros(qs_ref.shape, BF16)
    for h in range(N_HEADS):
        g, r = divmod(h, KV_GROUP)
        qs_ref[g, g * HEAD_DIM:(g + 1) * HEAD_DIM, r * tq:(r + 1) * tq] = qt_ref[0, h * HEAD_DIM:(h + 1) * HEAD_DIM, :]
    acc_ref[...] = jnp.zeros(acc_ref.shape, F32)
    if not bounded:
        m_ref[...] = jnp.full(m_ref.shape, -jnp.inf, F32)

    def body(c, carry):
        kc = k_ref[0, pl.ds(pl.multiple_of(c * tk, tk), tk), :]
        for g in range(N_KV_HEADS):
            st = jnp.dot(kc, qs_ref[g], preferred_element_type=F32)
            if bounded:
                pt = jnp.exp2(st).astype(BF16)
                acc_ref[g] += jnp.dot(vt_ref[0, c, g], pt, preferred_element_type=F32)
            else:
                m_old = m_ref[g]
                m_new = jnp.maximum(m_old, jnp.max(st, axis=0, keepdims=True))
                pt = jnp.exp2(st - m_new).astype(BF16)
                acc_ref[g] = (jnp.exp2(m_old - m_new) * acc_ref[g]
                              + jnp.dot(vt_ref[0, c, g], pt, preferred_element_type=F32))
                m_ref[g] = m_new
        return carry

    lax.fori_loop(0, nc, body, 0, unroll=4)
    for h in range(0, N_HEADS, 2):
        g, r = divmod(h, KV_GROUP)
        acc = acc_ref[g, :, r * tq:(r + 2) * tq]
        ot = acc[:HEAD_DIM] / acc[HEAD_DIM:]
        pair = jnp.concatenate([ot[:, :tq], ot[:, tq:]], axis=0)
        o_ref[0, :, h * HEAD_DIM:(h + 2) * HEAD_DIM] = pair.T.astype(BF16)


def _attention(qt, k, vt, tq, score_bound):
    b, _, l = qt.shape
    nc, tk = vt.shape[1], vt.shape[4]

    def call(bounded):
        return pl.pallas_call(
            functools.partial(_attn_kernel, bounded=bounded),
            grid=(b, l // tq),
            in_specs=[pl.BlockSpec((1, ATTN_W, tq), lambda i, j: (i, 0, j)),
                      pl.BlockSpec((1, l, KV_W), lambda i, j: (i, 0, 0)),
                      pl.BlockSpec((1, nc, N_KV_HEADS, 2 * HEAD_DIM, tk), lambda i, j: (i, 0, 0, 0, 0))],
            out_specs=pl.BlockSpec((1, tq, ATTN_W), lambda i, j: (i, j, 0)),
            out_shape=jax.ShapeDtypeStruct((b, l, ATTN_W), BF16),
            scratch_shapes=[pltpu.VMEM((N_KV_HEADS, KV_W, KV_GROUP * tq), BF16),
                            pltpu.VMEM((N_KV_HEADS, 2 * HEAD_DIM, KV_GROUP * tq), F32),
                            pltpu.VMEM((N_KV_HEADS, 1, KV_GROUP * tq), F32)],
            compiler_params=_cparams("parallel", "parallel"),
            name="attention" if bounded else "attention_running_max",
        )

    return lax.cond(score_bound <= SAFE_SCORE, call(True), call(False), qt, k, vt)


def _complex_powers(lr, li, n):
    pr, pi = jnp.ones_like(lr)[None], jnp.zeros_like(li)[None]
    cr, ci = lr, li
    while pr.shape[0] < n:
        pr, pi = (jnp.concatenate([pr, pr * cr - pi * ci]), jnp.concatenate([pi, pr * ci + pi * cr]))
        cr, ci = cr * cr - ci * ci, 2.0 * cr * ci
    return pr[:n], pi[:n]


def _s5_operators(a_re, a_im, log_dt, b_re, b_im, c_re, c_im, t):
    hi = lax.Precision.HIGHEST
    a_re, a_im, log_dt = a_re.astype(F32), a_im.astype(F32), log_dt.astype(F32)
    b_re, b_im, c_re, c_im = (z.astype(F32) for z in (b_re, b_im, c_re, c_im))
    dt = jnp.exp(log_dt)[..., None]
    mag = jnp.exp(a_re * dt)
    lam_re, lam_im = mag * jnp.cos(a_im * dt), mag * jnp.sin(a_im * dt)
    num_re = lam_re - 1.0
    den = a_re * a_re + a_im * a_im
    coef_re = (num_re * a_re + lam_im * a_im) / den
    coef_im = (lam_im * a_re - num_re * a_im) / den
    pw_re, pw_im = _complex_powers(lam_re, lam_im, t + 1)
    e_re = pw_re * coef_re - pw_im * coef_im
    e_im = pw_re * coef_im + pw_im * coef_re
    g, p, hch = b_re.shape

    cb_re = c_re[:, :, :, None] * b_re[:, None, :, :] - c_im[:, :, :, None] * b_im[:, None, :, :]
    cb_im = c_re[:, :, :, None] * b_im[:, None, :, :] + c_im[:, :, :, None] * b_re[:, None, :, :]
    kern = (jnp.einsum('tdgp,ghpk->dgthk', e_re[:t], cb_re, precision=hi)
            - jnp.einsum('tdgp,ghpk->dgthk', e_im[:t], cb_im, precision=hi))
    k_f, k_b = kern[0], kern[1]
    k_cat = jnp.concatenate([k_b[:, :0:-1], (k_f[:, :1] + k_b[:, :1]), k_f[:, 1:]], axis=1)
    k_cat = k_cat.transpose(0, 3, 2, 1)
    toep = jnp.stack([k_cat[..., t - 1 - s:2 * t - 1 - s] for s in range(t)], axis=2)
    toep = toep.reshape(g, hch * t, hch * t)

    def times_b(er, ei):
        re = er[:, :, :, None] * b_re[None] - ei[:, :, :, None] * b_im[None]
        im = er[:, :, :, None] * b_im[None] + ei[:, :, :, None] * b_re[None]
        f = lambda z: z.transpose(1, 3, 0, 2).reshape(g, hch * t, p)
        return f(re), f(im)
    sf_re, sf_im = times_b(e_re[:t, 0][::-1], e_im[:t, 0][::-1])
    sb_re, sb_im = times_b(e_re[:t, 1], e_im[:t, 1])
    w_s = jnp.concatenate([sf_re, sf_im, sb_re, sb_im], axis=2)

    def c_times(pr, pi):
        re = c_re[None] * pr[:, :, None, :] - c_im[None] * pi[:, :, None, :]
        im = c_re[None] * pi[:, :, None, :] + c_im[None] * pr[:, :, None, :]
        f = lambda z: z.transpose(1, 3, 2, 0).reshape(g, p, hch * t)
        return f(re), f(-im)
    of_re, of_im = c_times(pw_re[1:t + 1, 0], pw_im[1:t + 1, 0])
    ob_re, ob_im = c_times(pw_re[1:t + 1, 1][::-1], pw_im[1:t + 1, 1][::-1])
    w_o = jnp.concatenate([of_re, of_im, ob_re, ob_im], axis=1)

    lt_re, lt_im = pw_re[t], pw_im[t]
    a1 = jnp.concatenate([lt_re[0], lt_re[0], lt_re[1], lt_re[1]], axis=-1)[:, None, :]
    a2 = jnp.concatenate([-lt_im[0], lt_im[0], -lt_im[1], lt_im[1]], axis=-1)[:, None, :]
    return toep.astype(BF16), w_s.astype(BF16), w_o.astype(BF16), a1, a2


def _s5_chunk_rows(u_ref):
    b, hch, c, t = u_ref.shape
    return jnp.concatenate([u_ref[:, h].reshape(b * c, t) for h in range(hch)], axis=1)


def _s5_state_kernel(u_ref, ws_ref, s_ref):
    s_ref[0] = jnp.dot(_s5_chunk_rows(u_ref), ws_ref[0], preferred_element_type=F32)


def _s5_state(ut, w_s):
    b, ch, c, t = ut.shape
    g, k, n = w_s.shape
    return pl.pallas_call(
        _s5_state_kernel,
        grid=(g,),
        in_specs=[pl.BlockSpec((b, ch // g, c, t), lambda i: (0, i, 0, 0)),
                  pl.BlockSpec((1, k, n), lambda i: (i, 0, 0))],
        out_specs=pl.BlockSpec((1, b * c, n), lambda i: (i, 0, 0)),
        out_shape=jax.ShapeDtypeStruct((g, b * c, n), F32),
        compiler_params=_cparams("parallel"),
        name="s5_state",
    )(ut, w_s)


def _s5_scan_kernel(s_ref, a1_ref, a2_ref, h_ref, *, nb):
    gb, rows, _ = s_ref.shape
    nchunk = rows // nb
    half = 2 * SSM_STATE
    a1, a2 = a1_ref[...], a2_ref[...]
    a1f, a1b, a2f, a2b = a1[:, :, :half], a1[:, :, half:], a2[:, :, :half], a2[:, :, half:]

    def swap(z):
        return pltpu.roll(z.reshape(gb * nb, half), SSM_STATE, 1).reshape(gb, nb, half)

    def body(j, carry):
        hf, hb = carry
        rf = pl.ds(pl.multiple_of(j * nb, nb), nb)
        rb = pl.ds(pl.multiple_of((nchunk - 1 - j) * nb, nb), nb)
        h_ref[:, rf, :half] = hf
        h_ref[:, rb, half:] = hb
        hf = a1f * hf + a2f * swap(hf) + s_ref[:, rf, :half]
        hb = a1b * hb + a2b * swap(hb) + s_ref[:, rb, half:]
        return hf, hb

    zero = jnp.zeros((gb, nb, half), F32)
    lax.fori_loop(0, nchunk, body, (zero, zero))


def _s5_scan(s, a1, a2, nb, gb):
    g, r, n = s.shape
    blk = pl.BlockSpec((gb, r, n), lambda i: (i, 0, 0))
    par = pl.BlockSpec((gb, 1, n), lambda i: (i, 0, 0))
    return pl.pallas_call(
        functools.partial(_s5_scan_kernel, nb=nb),
        grid=(g // gb,),
        in_specs=[blk, par, par],
        out_specs=blk,
        out_shape=jax.ShapeDtypeStruct((g, r, n), F32),
        compiler_params=_cparams("parallel"),
        name="s5_scan",
    )(s, a1, a2)


def _s5_out_kernel(u_ref, h_ref, toep_ref, wo_ref, y_ref):
    b, hch, c, t = u_ref.shape
    y = jnp.dot(_s5_chunk_rows(u_ref), toep_ref[0], preferred_element_type=F32)
    y = y + jnp.dot(h_ref[0].astype(BF16), wo_ref[0], preferred_element_type=F32)
    for h in range(hch):
        y_ref[:, h] = y[:, h * t:(h + 1) * t].reshape(b, c, t)


def _s5_out(ut, hin, toep, w_o):
    b, ch, c, t = ut.shape
    g, n, k = w_o.shape
    blk = pl.BlockSpec((b, ch // g, c, t), lambda i: (0, i, 0, 0))
    return pl.pallas_call(
        _s5_out_kernel,
        grid=(g,),
        in_specs=[blk, pl.BlockSpec((1, b * c, n), lambda i: (i, 0, 0)),
                  pl.BlockSpec((1, k, k), lambda i: (i, 0, 0)),
                  pl.BlockSpec((1, n, k), lambda i: (i, 0, 0))],
        out_specs=blk,
        out_shape=jax.ShapeDtypeStruct((b, ch, c, t), F32),
        compiler_params=_cparams("parallel"),
        name="s5_out",
    )(ut, hin, toep, w_o)


def _s5_scan_branch(ut, ops):
    toep, w_s, w_o, a1, a2 = ops
    b, ch, l = ut.shape
    t = toep.shape[1] // SSM_GROUP_CH
    c = l // t
    g, n = w_s.shape[0], w_s.shape[2]
    ut = ut.reshape(b, ch, c, t)
    s = _s5_state(ut, w_s)
    s = s.reshape(g, b, c, n).transpose(0, 2, 1, 3).reshape(g, c * b, n)
    hin = _s5_scan(s, a1, a2, b, 4)
    hin = hin.reshape(g, c, b, n).transpose(0, 2, 1, 3).reshape(g, b * c, n)
    return _s5_out(ut, hin, toep, w_o).reshape(b, ch, l)


def _mem_kv_kernel(m_ref, g_ref, w_ref, o_ref):
    h = _rms(m_ref[0], g_ref[...]).astype(BF16)
    o_ref[0] = jnp.dot(h, w_ref[...], preferred_element_type=F32).astype(BF16)


def _mem_kv(mem, gain, w):
    b, m, d = mem.shape
    n = w.shape[1]
    return pl.pallas_call(
        _mem_kv_kernel,
        grid=(b,),
        in_specs=[pl.BlockSpec((1, m, d), lambda i: (i, 0, 0)), _const_spec((1, d)), _const_spec((d, n))],
        out_specs=pl.BlockSpec((1, m, n), lambda i: (i, 0, 0)),
        out_shape=jax.ShapeDtypeStruct((b, m, n), BF16),
        compiler_params=_cparams("parallel"),
        name="mem_kv",
    )(mem, gain, w)


def _merge_kernel(x_ref, a_ref, y_ref, u_ref, qc_ref, gate_ref, mkv_ref, dskip_ref, wglu_ref,
                  pa_ref, ps_ref, pc_ref, wout_ref, o_ref, *, sub):
    d = x_ref.shape[2]
    mv = mkv_ref[0, :, CROSS_W:]
    for r0 in range(0, x_ref.shape[1], sub):
        rows = slice(r0, r0 + sub)
        heads = []
        for h in range(N_CROSS_HEADS):
            sl = slice(h * HEAD_DIM, (h + 1) * HEAD_DIM)
            pair = slice((h // 2) * LANES, (h // 2 + 1) * LANES)
            s = lax.dot_general(_head_lanes(qc_ref[0, rows, pair], h), _head_lanes(mkv_ref[0, :, pair], h),
                                (((1,), (1,)), ((), ())), preferred_element_type=F32)
            p = jnp.exp(s - jnp.max(s, axis=-1, keepdims=True))
            den = jnp.sum(p, axis=-1, keepdims=True)
            o = jnp.dot(p.astype(BF16), mv, preferred_element_type=F32)
            heads.append(o[:, sl] / den)
        cross = jnp.concatenate(heads, axis=1).astype(BF16)
        y = y_ref[0, :, rows].T + dskip_ref[...] * u_ref[0, rows, :].astype(F32)
        y = jax.nn.gelu(y)
        y = y * jax.nn.sigmoid(jnp.dot(y.astype(BF16), wglu_ref[...], preferred_element_type=F32))
        gate = lambda j: gate_ref[0, rows, j * d:(j + 1) * d].astype(F32)
        m = gate(0) * jnp.dot(a_ref[0, rows, :], pa_ref[...], preferred_element_type=F32)
        m = m + gate(1) * jnp.dot(y.astype(BF16), ps_ref[...], preferred_element_type=F32)
        m = m + gate(2) * jnp.dot(cross, pc_ref[...], preferred_element_type=F32)
        o_ref[0, rows, :] = x_ref[0, rows, :] + jnp.dot(m.astype(BF16), wout_ref[...], preferred_element_type=F32)


def _merge(x, attn, yscan, u, qc, gates, mkv, d_skip, w_glu, p_attn, p_ssm, p_cross, w_out, tl):
    b, l, d = x.shape
    row = lambda width: pl.BlockSpec((1, tl, width), lambda i, j: (i, j, 0))
    n_mem = mkv.shape[1]
    return pl.pallas_call(
        functools.partial(_merge_kernel, sub=tl),
        grid=(b, l // tl),
        in_specs=[row(d), row(ATTN_W), pl.BlockSpec((1, SSM_W, tl), lambda i, j: (i, 0, j)),
                  row(SSM_W), row(CROSS_W), row(N_BRANCHES * d),
                  pl.BlockSpec((1, n_mem, 2 * CROSS_W), lambda i, j: (i, 0, 0)),
                  _const_spec((1, SSM_W)), _const_spec((SSM_W, SSM_W)),
                  _const_spec((ATTN_W, d)), _const_spec((SSM_W, d)), _const_spec((CROSS_W, d)),
                  _const_spec((d, d))],
        out_specs=row(d),
        out_shape=jax.ShapeDtypeStruct((b, l, d), F32),
        compiler_params=_cparams("parallel", "parallel"),
        name="merge",
    )(x, attn, yscan, u, qc, gates, mkv, d_skip, w_glu, p_attn, p_ssm, p_cross, w_out)


MXU_TILE = 256


def _ffn_chunks(dff, tiles_per_chunk=6):
    step = tiles_per_chunk * MXU_TILE
    return [min(step, dff - c0) for c0 in range(0, dff, step)]


def _ffn_kernel(xp_ref, x_ref, xn_ref, g_ref, wup_ref, cw_ref, cb_ref, wdn_ref, gf_ref, o_ref,
                h_scr, gate_scr, *, final_norm):
    j = pl.program_id(1)
    nj = pl.num_programs(1)
    tm = x_ref.shape[1]
    dff = wdn_ref.shape[0]
    gain = g_ref[...]
    x = x_ref[0]
    h_scr[0:HALO, :] = _rms(xp_ref[0], gain).astype(BF16)
    h_scr[HALO:HALO + tm, :] = _rms(x, gain).astype(BF16)
    h_scr[HALO + tm:, :] = _rms(xn_ref[0], gain).astype(BF16)
    y = x
    c0 = 0
    for wc in _ffn_chunks(dff):
        gs = gate_scr.at[:, c0:c0 + wc]
        gs[...] = jnp.dot(h_scr[...], wup_ref[:, c0:c0 + wc], preferred_element_type=F32)
        @pl.when(j == 0)
        def _():
            gs[0:HALO, :] = jnp.zeros((HALO, wc), F32)

        @pl.when(j == nj - 1)
        def _():
            gs[HALO + tm:, :] = jnp.zeros((HALO, wc), F32)

        val = jnp.dot(h_scr[HALO:HALO + tm, :], wup_ref[:, dff + c0:dff + c0 + wc], preferred_element_type=F32)
        cw = cw_ref[:, c0:c0 + wc]
        gc = (gs[HALO - 1:HALO - 1 + tm, :] * cw[0:1, :] + gs[HALO:HALO + tm, :] * cw[1:2, :]
              + gs[HALO + 1:HALO + 1 + tm, :] * cw[2:3, :] + cb_ref[:, c0:c0 + wc])
        act = (jax.nn.gelu(gc) * val).astype(BF16)
        y = y + jnp.dot(act, wdn_ref[c0:c0 + wc, :], preferred_element_type=F32)
        c0 += wc
    if final_norm:
        y = _rms(y, gf_ref[...])
    o_ref[0] = y


def _ffn(x, gain, w_up, conv_w, conv_b, w_down, final_gain, final_norm, tm):
    b, l, d = x.shape
    dff = w_down.shape[0]
    r = tm // HALO
    nh = l // HALO
    return pl.pallas_call(
        functools.partial(_ffn_kernel, final_norm=final_norm),
        grid=(b, l // tm),
        in_specs=[pl.BlockSpec((1, HALO, d), lambda i, j: (i, jnp.maximum(j * r - 1, 0), 0)),
                  pl.BlockSpec((1, tm, d), lambda i, j: (i, j, 0)),
                  pl.BlockSpec((1, HALO, d), lambda i, j: (i, jnp.minimum((j + 1) * r, nh - 1), 0)),
                  _const_spec((1, d)), _const_spec((d, 2 * dff)), _const_spec((3, dff)),
                  _const_spec((1, dff)), _const_spec((dff, d)), _const_spec((1, d))],
        out_specs=pl.BlockSpec((1, tm, d), lambda i, j: (i, j, 0)),
        out_shape=jax.ShapeDtypeStruct((b, l, d), F32),
        scratch_shapes=[pltpu.VMEM((tm + 2 * HALO, d), BF16),
                        pltpu.VMEM((tm + 2 * HALO, dff), F32)],
        compiler_params=_cparams("parallel", "arbitrary"),
        name="ffn",
    )(x, x, x, gain, w_up, conv_w, conv_b, w_down, final_gain)


def _rope_tables(l):
    rows = l // GRID_W
    r = jnp.broadcast_to(jnp.arange(rows, dtype=F32)[:, None], (rows, GRID_W)).reshape(l)
    c = jnp.broadcast_to(jnp.arange(GRID_W, dtype=F32)[None, :], (rows, GRID_W)).reshape(l)
    freqs = ROPE_THETA ** (-jnp.arange(ROPE_PAIRS, dtype=F32) / ROPE_PAIRS)
    ang_r, ang_c = r[:, None] * freqs, c[:, None] * freqs
    ang = jnp.concatenate([ang_r, ang_r, ang_c, ang_c], axis=-1)
    cos, sin = jnp.cos(ang), jnp.sin(ang)
    first = (jnp.arange(HEAD_DIM) % (2 * ROPE_PAIRS)) < ROPE_PAIRS
    sin_up = jnp.where(first, -sin, 0.0)
    sin_dn = jnp.where(first, 0.0, sin)
    two = lambda a: jnp.concatenate([a, a], axis=1)
    return two(cos), two(sin_up), two(sin_dn)


def _prepare_layer(p, l):
    bf = lambda a: a.astype(BF16)
    row = lambda a: a.astype(F32).reshape(1, -1)
    max_abs = lambda a: jnp.max(jnp.abs(a.astype(F32)))
    score_bound = 1.02 * QK_SCALE * HEAD_DIM * max_abs(p['q_norm'][l]) * max_abs(p['k_norm'][l])
    return dict(
        norm_mix=row(p['norm_mix'][l]), w_in=bf(p['w_in'][l]),
        q_gain=row(jnp.tile(p['q_norm'][l], N_HEADS)), k_gain=row(jnp.tile(p['k_norm'][l], N_KV_HEADS)),
        score_bound=score_bound,
        s5=_s5_operators(p['ssm_a_re'][l], p['ssm_a_im'][l], p['ssm_log_dt'][l], p['ssm_b_re'][l],
                         p['ssm_b_im'][l], p['ssm_c_re'][l], p['ssm_c_im'][l], S5_CHUNK),
        d_skip=row(p['ssm_d'][l]), w_glu=bf(p['ssm_glu'][l]),
        mem_norm=row(p['mem_norm'][l]), w_mem_kv=bf(p['w_mem_kv'][l]),
        p_attn=bf(p['p_attn'][l]), p_ssm=bf(p['p_ssm'][l]), p_cross=bf(p['p_cross'][l]), w_out=bf(p['w_out'][l]),
        norm_ffn=row(p['norm_ffn'][l]), w_up=bf(p['w_up'][l]), conv_w=p['conv_w'][l].astype(F32),
        conv_b=row(p['conv_b'][l]), w_down=bf(p['w_down'][l]),
    )


def _encode(x, mem, layers, final_gain):
    b, l, d = x.shape
    tables = _rope_tables(l)
    tk = _pick(l, 512)
    tq = _pick(l, 256)
    tm = _pick(l, 512)
    for li, w in enumerate(layers):
        qt, k, vt, u, ut, qc, gates = _in_proj(x, w['norm_mix'], w['w_in'], tables, w['q_gain'], w['k_gain'], tk)
        attn = _attention(qt, k, vt, tq, w['score_bound'])
        yscan = _s5_scan_branch(ut, w['s5'])
        mkv = _mem_kv(mem, w['mem_norm'], w['w_mem_kv'])
        x = _merge(x, attn, yscan, u, qc, gates, mkv, w['d_skip'], w['w_glu'], w['p_attn'], w['p_ssm'],
                   w['p_cross'], w['w_out'], tm)
        x = _ffn(x, w['norm_ffn'], w['w_up'], w['conv_w'], w['conv_b'], w['w_down'], final_gain,
                 li == len(layers) - 1, tm)
    return x


def kernel(x_prompt, x_sample, mem_prompt, mem_sample, norm_mix, w_in, q_norm, k_norm, ssm_a_re, ssm_a_im, ssm_log_dt, ssm_b_re, ssm_b_im, ssm_c_re, ssm_c_im, ssm_d, ssm_glu, mem_norm, w_mem_kv, p_attn, p_ssm, p_cross, w_out, norm_ffn, w_up, conv_w, conv_b, w_down, norm_final):
    p = dict(norm_mix=norm_mix, w_in=w_in, q_norm=q_norm, k_norm=k_norm, ssm_a_re=ssm_a_re, ssm_a_im=ssm_a_im,
             ssm_log_dt=ssm_log_dt, ssm_b_re=ssm_b_re, ssm_b_im=ssm_b_im, ssm_c_re=ssm_c_re, ssm_c_im=ssm_c_im,
             ssm_d=ssm_d, ssm_glu=ssm_glu, mem_norm=mem_norm, w_mem_kv=w_mem_kv, p_attn=p_attn, p_ssm=p_ssm,
             p_cross=p_cross, w_out=w_out, norm_ffn=norm_ffn, w_up=w_up, conv_w=conv_w, conv_b=conv_b,
             w_down=w_down)
    layers = [_prepare_layer(p, l) for l in range(norm_mix.shape[0])]
    final_gain = norm_final.astype(F32).reshape(1, -1)
    y_prompt = _encode(x_prompt, mem_prompt, layers, final_gain)
    y_sample = _encode(x_sample, mem_sample, layers, final_gain)
    return (y_prompt, y_sample)
```

```python
import functools
import math

import jax
import jax.numpy as jnp
from jax import lax
from jax.experimental import pallas as pl
from jax.experimental.pallas import tpu as pltpu

HEAD_DIM = 64
N_HEADS = 8
N_KV_HEADS = 2
KV_GROUP = N_HEADS // N_KV_HEADS
ATTN_W = N_HEADS * HEAD_DIM
KV_W = N_KV_HEADS * HEAD_DIM
SSM_GROUP_CH = 16
SSM_W = 256
SSM_GROUPS = SSM_W // SSM_GROUP_CH
SSM_STATE = 64
N_CROSS_HEADS = 4
CROSS_W = N_CROSS_HEADS * HEAD_DIM
N_BRANCHES = 3
GRID_W = 64
ROPE_THETA = 10000.0
ROPE_PAIRS = HEAD_DIM // 4
EPS = 1e-6

LOG2E = 1.4426950408889634
QK_SCALE = HEAD_DIM ** -0.5 * LOG2E
SAFE_SCORE = 64.0
LANES = 128
S5_CHUNK = LANES
PV_ROWS = HEAD_DIM + 16
HALO = 16
VMEM_LIMIT = 56 * 1024 * 1024

F32 = jnp.float32
BF16 = jnp.bfloat16


def _cparams(*sem):
    return pltpu.CompilerParams(dimension_semantics=sem, vmem_limit_bytes=VMEM_LIMIT)


def _const_spec(shape):
    nd = len(shape)
    return pl.BlockSpec(shape, lambda *_: (0,) * nd, pipeline_mode=pl.Buffered(1))


def _rms(x, gain):
    return x * lax.rsqrt(jnp.mean(x * x, axis=-1, keepdims=True) + EPS) * gain


def _pick(n, pref):
    t = min(n, pref)
    while n % t:
        t //= 2
    return t


def _head_norm_rope(x, gain, cos, sin_up, sin_dn, out_scale):
    w = x.shape[1]
    reps = w // LANES
    tile = lambda a: a if reps == 1 else jnp.concatenate([a] * reps, axis=1)
    xg = x * gain
    y = (xg * tile(cos)
         + pltpu.roll(xg, w - ROPE_PAIRS, 1) * tile(sin_up)
         + pltpu.roll(xg, ROPE_PAIRS, 1) * tile(sin_dn))
    lane = lax.broadcasted_iota(jnp.int32, (1, LANES), 1)
    low = lane < HEAD_DIM
    outs = []
    for p in range(reps):
        xp = x[:, p * LANES:(p + 1) * LANES]
        sq = xp * xp
        ss_all = jnp.sum(sq, axis=-1, keepdims=True)
        ss_lo = jnp.sum(jnp.where(low, sq, 0.0), axis=-1, keepdims=True)
        r_lo = lax.rsqrt(ss_lo * (1.0 / HEAD_DIM) + EPS) * out_scale
        r_hi = lax.rsqrt((ss_all - ss_lo) * (1.0 / HEAD_DIM) + EPS) * out_scale
        outs.append(y[:, p * LANES:(p + 1) * LANES] * jnp.where(low, r_lo, r_hi))
    return outs[0] if reps == 1 else jnp.concatenate(outs, axis=1)


def _in_proj_kernel(x_ref, g_ref, w_ref, cos_ref, sup_ref, sdn_ref, qg_ref, kg_ref,
                    qt_ref, k_ref, vt_ref, u_ref, ut_ref, qc_ref, gate_ref, h_scr):
    h_scr[...] = _rms(x_ref[0], g_ref[...]).astype(BF16)

    def seg(a, b):
        return jnp.dot(h_scr[...], w_ref[:, a:b], preferred_element_type=F32)

    cos, sup, sdn = cos_ref[...], sup_ref[...], sdn_ref[...]
    o = 0
    q = seg(o, o + ATTN_W)
    qt_ref[0] = _head_norm_rope(q, qg_ref[...], cos, sup, sdn, QK_SCALE).T.astype(BF16)
    o += ATTN_W
    kv = seg(o, o + 2 * KV_W)
    k_ref[0] = _head_norm_rope(kv[:, :KV_W], kg_ref[...], cos, sup, sdn, 1.0).astype(BF16)
    vt = kv[:, KV_W:].T.astype(BF16)
    ones = jnp.ones((PV_ROWS - HEAD_DIM, vt.shape[1]), BF16)
    for g in range(N_KV_HEADS):
        vt_ref[0, 0, g, :HEAD_DIM, :] = vt[g * HEAD_DIM:(g + 1) * HEAD_DIM, :]
        vt_ref[0, 0, g, HEAD_DIM:, :] = ones
    o += 2 * KV_W
    u = seg(o, o + SSM_W)
    u_ref[0] = u.astype(BF16)
    ut_ref[0] = u.T.astype(BF16)
    o += SSM_W
    qc_ref[0] = (seg(o, o + CROSS_W) * HEAD_DIM ** -0.5).astype(BF16)
    o += CROSS_W
    d = x_ref.shape[2]
    for j in range(N_BRANCHES):
        gate_ref[0, :, j * d:(j + 1) * d] = jax.nn.sigmoid(seg(o + j * d, o + (j + 1) * d)).astype(BF16)


def _in_proj(x, gain, w, tables, q_gain, k_gain, tk):
    b, l, d = x.shape
    n_in = w.shape[1]
    nj = l // tk
    row = lambda width: pl.BlockSpec((1, tk, width), lambda i, j: (i, j, 0))
    tab = pl.BlockSpec((tk, LANES), lambda i, j: (j, 0))
    out_shape = (
        jax.ShapeDtypeStruct((b, ATTN_W, l), BF16),
        jax.ShapeDtypeStruct((b, l, KV_W), BF16),
        jax.ShapeDtypeStruct((b, nj, N_KV_HEADS, PV_ROWS, tk), BF16),
        jax.ShapeDtypeStruct((b, l, SSM_W), BF16),
        jax.ShapeDtypeStruct((b, SSM_W, l), BF16),
        jax.ShapeDtypeStruct((b, l, CROSS_W), BF16),
        jax.ShapeDtypeStruct((b, l, N_BRANCHES * d), BF16),
    )
    return pl.pallas_call(
        _in_proj_kernel,
        grid=(b, nj),
        in_specs=[row(d), _const_spec((1, d)), _const_spec((d, n_in)), tab, tab, tab,
                  _const_spec((1, ATTN_W)), _const_spec((1, KV_W))],
        out_specs=(pl.BlockSpec((1, ATTN_W, tk), lambda i, j: (i, 0, j)), row(KV_W),
                   pl.BlockSpec((1, 1, N_KV_HEADS, PV_ROWS, tk), lambda i, j: (i, j, 0, 0, 0)),
                   row(SSM_W), pl.BlockSpec((1, SSM_W, tk), lambda i, j: (i, 0, j)),
                   row(CROSS_W), row(N_BRANCHES * d)),
        out_shape=out_shape,
        scratch_shapes=[pltpu.VMEM((tk, d), BF16)],
        compiler_params=_cparams("parallel", "parallel"),
        name="in_proj",
    )(x, gain, w, *tables, q_gain, k_gain)


def _head_lanes(pair, h):
    return pair[:, (h % 2) * HEAD_DIM:(h % 2 + 1) * HEAD_DIM]


def _attn_kernel(qt_ref, k_ref, vt_ref, o_ref, qs_ref, acc_ref, m_ref, *, bounded):
    tq = qt_ref.shape[2]
    nc, tk = vt_ref.shape[1], vt_ref.shape[4]
    qs_ref[...] = jnp.zeros(qs_ref.shape, BF16)
    for h in range(N_HEADS):
        g, r = divmod(h, KV_GROUP)
        qs_ref[g, g * HEAD_DIM:(g + 1) * HEAD_DIM, r * tq:(r + 1) * tq] = qt_ref[0, h * HEAD_DIM:(h + 1) * HEAD_DIM, :]
    acc_ref[...] = jnp.zeros(acc_ref.shape, F32)
    if not bounded:
        m_ref[...] = jnp.full(m_ref.shape, -jnp.inf, F32)

    def body(c, carry):
        kc = k_ref[0, pl.ds(pl.multiple_of(c * tk, tk), tk), :]
        for g in range(N_KV_HEADS):
            st = jnp.dot(kc, qs_ref[g], preferred_element_type=F32)
            if bounded:
                pt = jnp.exp2(st).astype(BF16)
                acc_ref[g] += jnp.dot(vt_ref[0, c, g], pt, preferred_element_type=F32)
            else:
                m_old = m_ref[g]
                m_new = jnp.maximum(m_old, jnp.max(st, axis=0, keepdims=True))
                pt = jnp.exp2(st - m_new).astype(BF16)
                acc_ref[g] = (jnp.exp2(m_old - m_new) * acc_ref[g]
                              + jnp.dot(vt_ref[0, c, g], pt, preferred_element_type=F32))
                m_ref[g] = m_new
        return carry

    lax.fori_loop(0, nc, body, 0, unroll=4)
    for h in range(0, N_HEADS, 2):
        g, r = divmod(h, KV_GROUP)
        acc = acc_ref[g, :, r * tq:(r + 2) * tq]
        ot = acc[:HEAD_DIM] / acc[HEAD_DIM:HEAD_DIM + 1]
        pair = jnp.concatenate([ot[:, :tq], ot[:, tq:]], axis=0)
        o_ref[0, :, h * HEAD_DIM:(h + 2) * HEAD_DIM] = pair.T.astype(BF16)


def _attention(qt, k, vt, tq, score_bound):
    b, _, l = qt.shape
    nc, tk = vt.shape[1], vt.shape[4]

    def call(bounded):
        return pl.pallas_call(
            functools.partial(_attn_kernel, bounded=bounded),
            grid=(b, l // tq),
            in_specs=[pl.BlockSpec((1, ATTN_W, tq), lambda i, j: (i, 0, j)),
                      pl.BlockSpec((1, l, KV_W), lambda i, j: (i, 0, 0)),
                      pl.BlockSpec((1, nc, N_KV_HEADS, PV_ROWS, tk), lambda i, j: (i, 0, 0, 0, 0))],
            out_specs=pl.BlockSpec((1, tq, ATTN_W), lambda i, j: (i, j, 0)),
            out_shape=jax.ShapeDtypeStruct((b, l, ATTN_W), BF16),
            scratch_shapes=[pltpu.VMEM((N_KV_HEADS, KV_W, KV_GROUP * tq), BF16),
                            pltpu.VMEM((N_KV_HEADS, PV_ROWS, KV_GROUP * tq), F32),
                            pltpu.VMEM((N_KV_HEADS, 1, KV_GROUP * tq), F32)],
            compiler_params=_cparams("parallel", "parallel"),
            name="attention" if bounded else "attention_running_max",
        )

    return lax.cond(score_bound <= SAFE_SCORE, call(True), call(False), qt, k, vt)


def _complex_powers(lr, li, n):
    pr, pi = jnp.ones_like(lr)[None], jnp.zeros_like(li)[None]
    cr, ci = lr, li
    while pr.shape[0] < n:
        pr, pi = (jnp.concatenate([pr, pr * cr - pi * ci]), jnp.concatenate([pi, pr * ci + pi * cr]))
        cr, ci = cr * cr - ci * ci, 2.0 * cr * ci
    return pr[:n], pi[:n]


def _toeplitz_kernel(k_ref, o_ref):
    hch, t = k_ref.shape[2], o_ref.shape[1]
    for h in range(hch):
        rows = jnp.broadcast_to(k_ref[0, 0, h:h + 1, :], (t, 2 * t))
        rolled = pltpu.roll(rows, t + 1, 1, stride=1, stride_axis=0)
        o_ref[0, :, h * t:(h + 1) * t] = rolled[:, :t].astype(BF16)


def _toeplitz(k_cat):
    g, hin, hout, t2 = k_cat.shape
    t = t2 // 2
    return pl.pallas_call(
        _toeplitz_kernel,
        grid=(g, hin),
        in_specs=[pl.BlockSpec((1, 1, hout, t2), lambda i, j: (i, j, 0, 0))],
        out_specs=pl.BlockSpec((1, t, hout * t), lambda i, j: (i, j, 0)),
        out_shape=jax.ShapeDtypeStruct((g, hin * t, hout * t), BF16),
        compiler_params=_cparams("parallel", "parallel"),
        name="s5_toeplitz",
    )(k_cat)


def _s5_operators(a_re, a_im, log_dt, b_re, b_im, c_re, c_im, t):
    hi = lax.Precision.HIGHEST
    a_re, a_im, log_dt = a_re.astype(F32), a_im.astype(F32), log_dt.astype(F32)
    b_re, b_im, c_re, c_im = (z.astype(F32) for z in (b_re, b_im, c_re, c_im))
    dt = jnp.exp(log_dt)[..., None]
    mag = jnp.exp(a_re * dt)
    lam_re, lam_im = mag * jnp.cos(a_im * dt), mag * jnp.sin(a_im * dt)
    num_re = lam_re - 1.0
    den = a_re * a_re + a_im * a_im
    coef_re = (num_re * a_re + lam_im * a_im) / den
    coef_im = (lam_im * a_re - num_re * a_im) / den
    pw_re, pw_im = _complex_powers(lam_re, lam_im, t + 1)
    e_re = pw_re * coef_re - pw_im * coef_im
    e_im = pw_re * coef_im + pw_im * coef_re
    g, p, hch = b_re.shape

    cb_re = c_re[:, :, :, None] * b_re[:, None, :, :] - c_im[:, :, :, None] * b_im[:, None, :, :]
    cb_im = c_re[:, :, :, None] * b_im[:, None, :, :] + c_im[:, :, :, None] * b_re[:, None, :, :]
    kern = (jnp.einsum('tdgp,ghpk->dgthk', e_re[:t], cb_re, precision=hi)
            - jnp.einsum('tdgp,ghpk->dgthk', e_im[:t], cb_im, precision=hi))
    k_f, k_b = kern[0], kern[1]
    k_cat = jnp.concatenate([k_b[:, :0:-1], (k_f[:, :1] + k_b[:, :1]), k_f[:, 1:]], axis=1)
    k_cat = jnp.pad(k_cat.transpose(0, 3, 2, 1), ((0, 0), (0, 0), (0, 0), (0, 1)))
    toep = _toeplitz(k_cat)

    def times_b(er, ei):
        re = er[:, :, :, None] * b_re[None] - ei[:, :, :, None] * b_im[None]
        im = er[:, :, :, None] * b_im[None] + ei[:, :, :, None] * b_re[None]
        f = lambda z: z.transpose(1, 3, 0, 2).reshape(g, hch * t, p)
        return f(re), f(im)
    sf_re, sf_im = times_b(e_re[:t, 0][::-1], e_im[:t, 0][::-1])
    sb_re, sb_im = times_b(e_re[:t, 1], e_im[:t, 1])
    w_s = jnp.concatenate([sf_re, sf_im, sb_re, sb_im], axis=2)

    def c_times(pr, pi):
        re = c_re[None] * pr[:, :, None, :] - c_im[None] * pi[:, :, None, :]
        im = c_re[None] * pi[:, :, None, :] + c_im[None] * pr[:, :, None, :]
        f = lambda z: z.transpose(1, 3, 2, 0).reshape(g, p, hch * t)
        return f(re), f(-im)
    of_re, of_im = c_times(pw_re[1:t + 1, 0], pw_im[1:t + 1, 0])
    ob_re, ob_im = c_times(pw_re[1:t + 1, 1][::-1], pw_im[1:t + 1, 1][::-1])
    w_o = jnp.concatenate([of_re, of_im, ob_re, ob_im], axis=1)

    lt_re, lt_im = pw_re[t], pw_im[t]
    a1 = jnp.concatenate([lt_re[0], lt_re[0], lt_re[1], lt_re[1]], axis=-1)[:, None, :]
    a2 = jnp.concatenate([-lt_im[0], lt_im[0], -lt_im[1], lt_im[1]], axis=-1)[:, None, :]
    return toep.astype(BF16), w_s.astype(BF16), w_o.astype(BF16), a1, a2


def _s5_chunk_rows(u_ref):
    b, hch, c, t = u_ref.shape
    return jnp.concatenate([u_ref[:, h].reshape(b * c, t) for h in range(hch)], axis=1)


def _s5_state_kernel(u_ref, ws_ref, s_ref):
    s_ref[0] = jnp.dot(_s5_chunk_rows(u_ref), ws_ref[0], preferred_element_type=F32)


def _s5_state(ut, w_s):
    b, ch, c, t = ut.shape
    g, k, n = w_s.shape
    return pl.pallas_call(
        _s5_state_kernel,
        grid=(g,),
        in_specs=[pl.BlockSpec((b, ch // g, c, t), lambda i: (0, i, 0, 0)),
                  pl.BlockSpec((1, k, n), lambda i: (i, 0, 0))],
        out_specs=pl.BlockSpec((1, b * c, n), lambda i: (i, 0, 0)),
        out_shape=jax.ShapeDtypeStruct((g, b * c, n), F32),
        compiler_params=_cparams("parallel"),
        name="s5_state",
    )(ut, w_s)


def _s5_scan_kernel(s_ref, a1_ref, a2_ref, h_ref, *, nb):
    gb, rows, _ = s_ref.shape
    nchunk = rows // nb
    half = 2 * SSM_STATE
    a1, a2 = a1_ref[...], a2_ref[...]
    a1f, a1b, a2f, a2b = a1[:, :, :half], a1[:, :, half:], a2[:, :, :half], a2[:, :, half:]

    def swap(z):
        return pltpu.roll(z.reshape(gb * nb, half), SSM_STATE, 1).reshape(gb, nb, half)

    def body(j, carry):
        hf, hb = carry
        rf = pl.ds(pl.multiple_of(j * nb, nb), nb)
        rb = pl.ds(pl.multiple_of((nchunk - 1 - j) * nb, nb), nb)
        h_ref[:, rf, :half] = hf
        h_ref[:, rb, half:] = hb
        hf = a1f * hf + a2f * swap(hf) + s_ref[:, rf, :half]
        hb = a1b * hb + a2b * swap(hb) + s_ref[:, rb, half:]
        return hf, hb

    zero = jnp.zeros((gb, nb, half), F32)
    lax.fori_loop(0, nchunk, body, (zero, zero))


def _s5_scan(s, a1, a2, nb, gb):
    g, r, n = s.shape
    blk = pl.BlockSpec((gb, r, n), lambda i: (i, 0, 0))
    par = pl.BlockSpec((gb, 1, n), lambda i: (i, 0, 0))
    return pl.pallas_call(
        functools.partial(_s5_scan_kernel, nb=nb),
        grid=(g // gb,),
        in_specs=[blk, par, par],
        out_specs=blk,
        out_shape=jax.ShapeDtypeStruct((g, r, n), F32),
        compiler_params=_cparams("parallel"),
        name="s5_scan",
    )(s, a1, a2)


def _s5_out_kernel(u_ref, h_ref, toep_ref, wo_ref, y_ref):
    b, hch, c, t = u_ref.shape
    y = jnp.dot(_s5_chunk_rows(u_ref), toep_ref[0], preferred_element_type=F32)
    y = y + jnp.dot(h_ref[0].astype(BF16), wo_ref[0], preferred_element_type=F32)
    for h in range(hch):
        y_ref[:, h] = y[:, h * t:(h + 1) * t].reshape(b, c, t)


def _s5_out(ut, hin, toep, w_o):
    b, ch, c, t = ut.shape
    g, n, k = w_o.shape
    blk = pl.BlockSpec((b, ch // g, c, t), lambda i: (0, i, 0, 0))
    return pl.pallas_call(
        _s5_out_kernel,
        grid=(g,),
        in_specs=[blk, pl.BlockSpec((1, b * c, n), lambda i: (i, 0, 0)),
                  pl.BlockSpec((1, k, k), lambda i: (i, 0, 0)),
                  pl.BlockSpec((1, n, k), lambda i: (i, 0, 0))],
        out_specs=blk,
        out_shape=jax.ShapeDtypeStruct((b, ch, c, t), F32),
        compiler_params=_cparams("parallel"),
        name="s5_out",
    )(ut, hin, toep, w_o)


def _s5_scan_branch(ut, ops):
    toep, w_s, w_o, a1, a2 = ops
    b, ch, l = ut.shape
    t = toep.shape[1] // SSM_GROUP_CH
    c = l // t
    g, n = w_s.shape[0], w_s.shape[2]
    ut = ut.reshape(b, ch, c, t)
    s = _s5_state(ut, w_s)
    s = s.reshape(g, b, c, n).transpose(0, 2, 1, 3).reshape(g, c * b, n)
    hin = _s5_scan(s, a1, a2, b, 4)
    hin = hin.reshape(g, c, b, n).transpose(0, 2, 1, 3).reshape(g, b * c, n)
    return _s5_out(ut, hin, toep, w_o).reshape(b, ch, l)


def _mem_kv_kernel(m_ref, g_ref, w_ref, o_ref):
    h = _rms(m_ref[0], g_ref[...]).astype(BF16)
    o_ref[0] = jnp.dot(h, w_ref[...], preferred_element_type=F32).astype(BF16)


def _mem_kv(mem, gain, w):
    b, m, d = mem.shape
    n = w.shape[1]
    return pl.pallas_call(
        _mem_kv_kernel,
        grid=(b,),
        in_specs=[pl.BlockSpec((1, m, d), lambda i: (i, 0, 0)), _const_spec((1, d)), _const_spec((d, n))],
        out_specs=pl.BlockSpec((1, m, n), lambda i: (i, 0, 0)),
        out_shape=jax.ShapeDtypeStruct((b, m, n), BF16),
        compiler_params=_cparams("parallel"),
        name="mem_kv",
    )(mem, gain, w)


def _merge_kernel(x_ref, a_ref, y_ref, u_ref, qc_ref, gate_ref, mkv_ref, dskip_ref, wglu_ref,
                  pa_ref, ps_ref, pc_ref, wout_ref, o_ref, *, sub):
    d = x_ref.shape[2]
    mv = mkv_ref[0, :, CROSS_W:]
    for r0 in range(0, x_ref.shape[1], sub):
        rows = slice(r0, r0 + sub)
        heads = []
        for h in range(N_CROSS_HEADS):
            sl = slice(h * HEAD_DIM, (h + 1) * HEAD_DIM)
            pair = slice((h // 2) * LANES, (h // 2 + 1) * LANES)
            s = lax.dot_general(_head_lanes(qc_ref[0, rows, pair], h), _head_lanes(mkv_ref[0, :, pair], h),
                                (((1,), (1,)), ((), ())), preferred_element_type=F32)
            p = jnp.exp(s - jnp.max(s, axis=-1, keepdims=True))
            den = jnp.sum(p, axis=-1, keepdims=True)
            o = jnp.dot(p.astype(BF16), mv, preferred_element_type=F32)
            heads.append(o[:, sl] / den)
        cross = jnp.concatenate(heads, axis=1).astype(BF16)
        y = y_ref[0, :, rows].T + dskip_ref[...] * u_ref[0, rows, :].astype(F32)
        y = jax.nn.gelu(y)
        y = y * jax.nn.sigmoid(jnp.dot(y.astype(BF16), wglu_ref[...], preferred_element_type=F32))
        gate = lambda j: gate_ref[0, rows, j * d:(j + 1) * d].astype(F32)
        m = gate(0) * jnp.dot(a_ref[0, rows, :], pa_ref[...], preferred_element_type=F32)
        m = m + gate(1) * jnp.dot(y.astype(BF16), ps_ref[...], preferred_element_type=F32)
        m = m + gate(2) * jnp.dot(cross, pc_ref[...], preferred_element_type=F32)
        o_ref[0, rows, :] = x_ref[0, rows, :] + jnp.dot(m.astype(BF16), wout_ref[...], preferred_element_type=F32)


def _merge(x, attn, yscan, u, qc, gates, mkv, d_skip, w_glu, p_attn, p_ssm, p_cross, w_out, tl):
    b, l, d = x.shape
    row = lambda width: pl.BlockSpec((1, tl, width), lambda i, j: (i, j, 0))
    n_mem = mkv.shape[1]
    return pl.pallas_call(
        functools.partial(_merge_kernel, sub=tl),
        grid=(b, l // tl),
        in_specs=[row(d), row(ATTN_W), pl.BlockSpec((1, SSM_W, tl), lambda i, j: (i, 0, j)),
                  row(SSM_W), row(CROSS_W), row(N_BRANCHES * d),
                  pl.BlockSpec((1, n_mem, 2 * CROSS_W), lambda i, j: (i, 0, 0)),
                  _const_spec((1, SSM_W)), _const_spec((SSM_W, SSM_W)),
                  _const_spec((ATTN_W, d)), _const_spec((SSM_W, d)), _const_spec((CROSS_W, d)),
                  _const_spec((d, d))],
        out_specs=row(d),
        out_shape=jax.ShapeDtypeStruct((b, l, d), F32),
        compiler_params=_cparams("parallel", "parallel"),
        name="merge",
    )(x, attn, yscan, u, qc, gates, mkv, d_skip, w_glu, p_attn, p_ssm, p_cross, w_out)


MXU_TILE = 256


def _ffn_chunks(dff, tiles_per_chunk=11):
    step = tiles_per_chunk * MXU_TILE
    return [min(step, dff - c0) for c0 in range(0, dff, step)]


def _ffn_kernel(xp_ref, x_ref, xn_ref, g_ref, wup_ref, cw_ref, cb_ref, wdn_ref, gf_ref, o_ref,
                h_scr, gate_scr, *, final_norm):
    j = pl.program_id(1)
    nj = pl.num_programs(1)
    tm = x_ref.shape[1]
    dff = wdn_ref.shape[0]
    gain = g_ref[...]
    x = x_ref[0]
    h_scr[0:HALO, :] = jnp.where(j == 0, 0.0, _rms(xp_ref[0], gain)).astype(BF16)
    h_scr[HALO:HALO + tm, :] = _rms(x, gain).astype(BF16)
    h_scr[HALO + tm:, :] = jnp.where(j == nj - 1, 0.0, _rms(xn_ref[0], gain)).astype(BF16)
    y = x
    c0 = 0
    for wc in _ffn_chunks(dff):
        gs = gate_scr.at[:, c0:c0 + wc]
        gs[...] = jnp.dot(h_scr[...], wup_ref[:, c0:c0 + wc], preferred_element_type=F32)
        val = jnp.dot(h_scr[HALO:HALO + tm, :], wup_ref[:, dff + c0:dff + c0 + wc], preferred_element_type=F32)
        cw = cw_ref[:, c0:c0 + wc]
        gc = (gs[HALO - 1:HALO - 1 + tm, :] * cw[0:1, :] + gs[HALO:HALO + tm, :] * cw[1:2, :]
              + gs[HALO + 1:HALO + 1 + tm, :] * cw[2:3, :] + cb_ref[:, c0:c0 + wc])
        act = (jax.nn.gelu(gc) * val).astype(BF16)
        y = y + jnp.dot(act, wdn_ref[c0:c0 + wc, :], preferred_element_type=F32)
        c0 += wc
    if final_norm:
        y = _rms(y, gf_ref[...])
    o_ref[0] = y


def _ffn(x, gain, w_up, conv_w, conv_b, w_down, final_gain, final_norm, tm):
    b, l, d = x.shape
    dff = w_down.shape[0]
    r = tm // HALO
    nh = l // HALO
    return pl.pallas_call(
        functools.partial(_ffn_kernel, final_norm=final_norm),
        grid=(b, l // tm),
        in_specs=[pl.BlockSpec((1, HALO, d), lambda i, j: (i, jnp.maximum(j * r - 1, 0), 0)),
                  pl.BlockSpec((1, tm, d), lambda i, j: (i, j, 0)),
                  pl.BlockSpec((1, HALO, d), lambda i, j: (i, jnp.minimum((j + 1) * r, nh - 1), 0)),
                  _const_spec((1, d)), _const_spec((d, 2 * dff)), _const_spec((3, dff)),
                  _const_spec((1, dff)), _const_spec((dff, d)), _const_spec((1, d))],
        out_specs=pl.BlockSpec((1, tm, d), lambda i, j: (i, j, 0)),
        out_shape=jax.ShapeDtypeStruct((b, l, d), F32),
        scratch_shapes=[pltpu.VMEM((tm + 2 * HALO, d), BF16),
                        pltpu.VMEM((tm + 2 * HALO, dff), F32)],
        compiler_params=_cparams("parallel", "arbitrary"),
        name="ffn",
    )(x, x, x, gain, w_up, conv_w, conv_b, w_down, final_gain)


def _rope_tables(l):
    rows = l // GRID_W
    r = jnp.broadcast_to(jnp.arange(rows, dtype=F32)[:, None], (rows, GRID_W)).reshape(l)
    c = jnp.broadcast_to(jnp.arange(GRID_W, dtype=F32)[None, :], (rows, GRID_W)).reshape(l)
    freqs = ROPE_THETA ** (-jnp.arange(ROPE_PAIRS, dtype=F32) / ROPE_PAIRS)
    ang_r, ang_c = r[:, None] * freqs, c[:, None] * freqs
    ang = jnp.concatenate([ang_r, ang_r, ang_c, ang_c], axis=-1)
    cos, sin = jnp.cos(ang), jnp.sin(ang)
    first = (jnp.arange(HEAD_DIM) % (2 * ROPE_PAIRS)) < ROPE_PAIRS
    sin_up = jnp.where(first, -sin, 0.0)
    sin_dn = jnp.where(first, 0.0, sin)
    two = lambda a: jnp.concatenate([a, a], axis=1)
    return two(cos), two(sin_up), two(sin_dn)


def _prepare_layer(p, l):
    bf = lambda a: a.astype(BF16)
    row = lambda a: a.astype(F32).reshape(1, -1)
    max_abs = lambda a: jnp.max(jnp.abs(a.astype(F32)))
    score_bound = 1.02 * QK_SCALE * HEAD_DIM * max_abs(p['q_norm'][l]) * max_abs(p['k_norm'][l])
    return dict(
        norm_mix=row(p['norm_mix'][l]), w_in=bf(p['w_in'][l]),
        q_gain=row(jnp.tile(p['q_norm'][l], N_HEADS)), k_gain=row(jnp.tile(p['k_norm'][l], N_KV_HEADS)),
        score_bound=score_bound,
        s5=_s5_operators(p['ssm_a_re'][l], p['ssm_a_im'][l], p['ssm_log_dt'][l], p['ssm_b_re'][l],
                         p['ssm_b_im'][l], p['ssm_c_re'][l], p['ssm_c_im'][l], S5_CHUNK),
        d_skip=row(p['ssm_d'][l]), w_glu=bf(p['ssm_glu'][l]),
        mem_norm=row(p['mem_norm'][l]), w_mem_kv=bf(p['w_mem_kv'][l]),
        p_attn=bf(p['p_attn'][l]), p_ssm=bf(p['p_ssm'][l]), p_cross=bf(p['p_cross'][l]), w_out=bf(p['w_out'][l]),
        norm_ffn=row(p['norm_ffn'][l]), w_up=bf(p['w_up'][l]), conv_w=p['conv_w'][l].astype(F32),
        conv_b=row(p['conv_b'][l]), w_down=bf(p['w_down'][l]),
    )


def _encode(x, mem, layers, final_gain):
    b, l, d = x.shape
    tables = _rope_tables(l)
    tk = _pick(l, 512)
    tq = _pick(l, 256)
    tm = _pick(l, 512)
    for li, w in enumerate(layers):
        qt, k, vt, u, ut, qc, gates = _in_proj(x, w['norm_mix'], w['w_in'], tables, w['q_gain'], w['k_gain'], tk)
        attn = _attention(qt, k, vt, tq, w['score_bound'])
        yscan = _s5_scan_branch(ut, w['s5'])
        mkv = _mem_kv(mem, w['mem_norm'], w['w_mem_kv'])
        x = _merge(x, attn, yscan, u, qc, gates, mkv, w['d_skip'], w['w_glu'], w['p_attn'], w['p_ssm'],
                   w['p_cross'], w['w_out'], tm)
        x = _ffn(x, w['norm_ffn'], w['w_up'], w['conv_w'], w['conv_b'], w['w_down'], final_gain,
                 li == len(layers) - 1, tm)
    return x


def kernel(x_prompt, x_sample, mem_prompt, mem_sample, norm_mix, w_in, q_norm, k_norm, ssm_a_re, ssm_a_im, ssm_log_dt, ssm_b_re, ssm_b_im, ssm_c_re, ssm_c_im, ssm_d, ssm_glu, mem_norm, w_mem_kv, p_attn, p_ssm, p_cross, w_out, norm_ffn, w_up, conv_w, conv_b, w_down, norm_final):
    p = dict(norm_mix=norm_mix, w_in=w_in, q_norm=q_norm, k_norm=k_norm, ssm_a_re=ssm_a_re, ssm_a_im=ssm_a_im,
             ssm_log_dt=ssm_log_dt, ssm_b_re=ssm_b_re, ssm_b_im=ssm_b_im, ssm_c_re=ssm_c_re, ssm_c_im=ssm_c_im,
             ssm_d=ssm_d, ssm_glu=ssm_glu, mem_norm=mem_norm, w_mem_kv=w_mem_kv, p_attn=p_attn, p_ssm=p_ssm,
             p_cross=p_cross, w_out=w_out, norm_ffn=norm_ffn, w_up=w_up, conv_w=conv_w, conv_b=conv_b,
             w_down=w_down)
    layers = [_prepare_layer(p, l) for l in range(norm_mix.shape[0])]
    final_gain = norm_final.astype(F32).reshape(1, -1)
    y_prompt = _encode(x_prompt, mem_prompt, layers, final_gain)
    y_sample = _encode(x_sample, mem_sample, layers, final_gain)
    return (y_prompt, y_sample)
```

```python
import functools
import math

import jax
import jax.numpy as jnp
from jax import lax
from jax.experimental import pallas as pl
from jax.experimental.pallas import tpu as pltpu

HEAD_DIM = 64
N_HEADS = 8
N_KV_HEADS = 2
KV_GROUP = N_HEADS // N_KV_HEADS
ATTN_W = N_HEADS * HEAD_DIM
KV_W = N_KV_HEADS * HEAD_DIM
SSM_GROUP_CH = 16
SSM_W = 256
SSM_GROUPS = SSM_W // SSM_GROUP_CH
SSM_STATE = 64
N_CROSS_HEADS = 4
CROSS_W = N_CROSS_HEADS * HEAD_DIM
N_BRANCHES = 3
GRID_W = 64
ROPE_THETA = 10000.0
ROPE_PAIRS = HEAD_DIM // 4
EPS = 1e-6

LOG2E = 1.4426950408889634
QK_SCALE = HEAD_DIM ** -0.5 * LOG2E
SAFE_SCORE = 64.0
LANES = 128
S5_CHUNK = LANES
PV_ROWS = 2 * HEAD_DIM
HALO = 16
VMEM_LIMIT = 56 * 1024 * 1024

F32 = jnp.float32
BF16 = jnp.bfloat16


def _cparams(*sem):
    return pltpu.CompilerParams(dimension_semantics=sem, vmem_limit_bytes=VMEM_LIMIT)


def _const_spec(shape):
    nd = len(shape)
    return pl.BlockSpec(shape, lambda *_: (0,) * nd, pipeline_mode=pl.Buffered(1))


def _rms(x, gain):
    return x * lax.rsqrt(jnp.mean(x * x, axis=-1, keepdims=True) + EPS) * gain


def _pick(n, pref):
    t = min(n, pref)
    while n % t:
        t //= 2
    return t


def _head_norm_rope(x, gain, cos, sin_up, sin_dn, out_scale):
    w = x.shape[1]
    reps = w // LANES
    tile = lambda a: a if reps == 1 else jnp.concatenate([a] * reps, axis=1)
    xg = x * gain
    y = (xg * tile(cos)
         + pltpu.roll(xg, w - ROPE_PAIRS, 1) * tile(sin_up)
         + pltpu.roll(xg, ROPE_PAIRS, 1) * tile(sin_dn))
    lane = lax.broadcasted_iota(jnp.int32, (1, LANES), 1)
    low = lane < HEAD_DIM
    outs = []
    for p in range(reps):
        xp = x[:, p * LANES:(p + 1) * LANES]
        sq = xp * xp
        ss_all = jnp.sum(sq, axis=-1, keepdims=True)
        ss_lo = jnp.sum(jnp.where(low, sq, 0.0), axis=-1, keepdims=True)
        r_lo = lax.rsqrt(ss_lo * (1.0 / HEAD_DIM) + EPS) * out_scale
        r_hi = lax.rsqrt((ss_all - ss_lo) * (1.0 / HEAD_DIM) + EPS) * out_scale
        outs.append(y[:, p * LANES:(p + 1) * LANES] * jnp.where(low, r_lo, r_hi))
    return outs[0] if reps == 1 else jnp.concatenate(outs, axis=1)


def _in_proj_kernel(x_ref, g_ref, w_ref, cos_ref, sup_ref, sdn_ref, qg_ref, kg_ref,
                    qt_ref, k_ref, vt_ref, u_ref, ut_ref, qc_ref, gate_ref, h_scr):
    h_scr[...] = _rms(x_ref[0], g_ref[...]).astype(BF16)
    z = jnp.dot(h_scr[...], w_ref[...], preferred_element_type=F32)

    def seg(a, b):
        return z[:, a:b]

    cos, sup, sdn = cos_ref[...], sup_ref[...], sdn_ref[...]
    o = 0
    q = seg(o, o + ATTN_W)
    qt_ref[0] = _head_norm_rope(q, qg_ref[...], cos, sup, sdn, QK_SCALE).T.astype(BF16)
    o += ATTN_W
    kv = seg(o, o + 2 * KV_W)
    k_ref[0] = _head_norm_rope(kv[:, :KV_W], kg_ref[...], cos, sup, sdn, 1.0).astype(BF16)
    vt = kv[:, KV_W:].T.astype(BF16)
    ones = jnp.ones((PV_ROWS - HEAD_DIM, vt.shape[1]), BF16)
    for g in range(N_KV_HEADS):
        vt_ref[0, 0, g, :HEAD_DIM, :] = vt[g * HEAD_DIM:(g + 1) * HEAD_DIM, :]
        vt_ref[0, 0, g, HEAD_DIM:, :] = ones
    o += 2 * KV_W
    u = seg(o, o + SSM_W)
    u_ref[0] = u.astype(BF16)
    ut_ref[0] = u.T.astype(BF16)
    o += SSM_W
    qc_ref[0] = (seg(o, o + CROSS_W) * HEAD_DIM ** -0.5).astype(BF16)
    o += CROSS_W
    d = x_ref.shape[2]
    for j in range(N_BRANCHES):
        gate_ref[0, :, j * d:(j + 1) * d] = jax.nn.sigmoid(seg(o + j * d, o + (j + 1) * d)).astype(BF16)


def _in_proj(x, gain, w, tables, q_gain, k_gain, tk):
    b, l, d = x.shape
    n_in = w.shape[1]
    nj = l // tk
    row = lambda width: pl.BlockSpec((1, tk, width), lambda i, j: (i, j, 0))
    tab = pl.BlockSpec((tk, LANES), lambda i, j: (j, 0))
    out_shape = (
        jax.ShapeDtypeStruct((b, ATTN_W, l), BF16),
        jax.ShapeDtypeStruct((b, l, KV_W), BF16),
        jax.ShapeDtypeStruct((b, nj, N_KV_HEADS, PV_ROWS, tk), BF16),
        jax.ShapeDtypeStruct((b, l, SSM_W), BF16),
        jax.ShapeDtypeStruct((b, SSM_W, l), BF16),
        jax.ShapeDtypeStruct((b, l, CROSS_W), BF16),
        jax.ShapeDtypeStruct((b, l, N_BRANCHES * d), BF16),
    )
    return pl.pallas_call(
        _in_proj_kernel,
        grid=(b, nj),
        in_specs=[row(d), _const_spec((1, d)), _const_spec((d, n_in)), tab, tab, tab,
                  _const_spec((1, ATTN_W)), _const_spec((1, KV_W))],
        out_specs=(pl.BlockSpec((1, ATTN_W, tk), lambda i, j: (i, 0, j)), row(KV_W),
                   pl.BlockSpec((1, 1, N_KV_HEADS, PV_ROWS, tk), lambda i, j: (i, j, 0, 0, 0)),
                   row(SSM_W), pl.BlockSpec((1, SSM_W, tk), lambda i, j: (i, 0, j)),
                   row(CROSS_W), row(N_BRANCHES * d)),
        out_shape=out_shape,
        scratch_shapes=[pltpu.VMEM((tk, d), BF16)],
        compiler_params=_cparams("parallel", "parallel"),
        name="in_proj",
    )(x, gain, w, *tables, q_gain, k_gain)


def _head_lanes(pair, h):
    return pair[:, (h % 2) * HEAD_DIM:(h % 2 + 1) * HEAD_DIM]


def _attn_kernel(qt_ref, k_ref, vt_ref, o_ref, qs_ref, acc_ref, m_ref, *, bounded):
    tq = qt_ref.shape[2]
    nc, tk = vt_ref.shape[1], vt_ref.shape[4]
    qs_ref[...] = jnp.zeros(qs_ref.shape, BF16)
    for h in range(N_HEADS):
        g, r = divmod(h, KV_GROUP)
        qs_ref[g, g * HEAD_DIM:(g + 1) * HEAD_DIM, r * tq:(r + 1) * tq] = qt_ref[0, h * HEAD_DIM:(h + 1) * HEAD_DIM, :]
    acc_ref[...] = jnp.zeros(acc_ref.shape, F32)
    if not bounded:
        m_ref[...] = jnp.full(m_ref.shape, -jnp.inf, F32)

    def body(c, carry):
        kc = k_ref[0, pl.ds(pl.multiple_of(c * tk, tk), tk), :]
        for g in range(N_KV_HEADS):
            st = jnp.dot(kc, qs_ref[g], preferred_element_type=F32)
            if bounded:
                pt = jnp.exp2(st).astype(BF16)
                acc_ref[g] += jnp.dot(vt_ref[0, c, g], pt, preferred_element_type=F32)
            else:
                m_old = m_ref[g]
                m_new = jnp.maximum(m_old, jnp.max(st, axis=0, keepdims=True))
                pt = jnp.exp2(st - m_new).astype(BF16)
                acc_ref[g] = (jnp.exp2(m_old - m_new) * acc_ref[g]
                              + jnp.dot(vt_ref[0, c, g], pt, preferred_element_type=F32))
                m_ref[g] = m_new
        return carry

    lax.fori_loop(0, nc, body, 0, unroll=4)
    for h in range(0, N_HEADS, 2):
        g, r = divmod(h, KV_GROUP)
        acc = acc_ref[g, :, r * tq:(r + 2) * tq]
        ot = acc[:HEAD_DIM] / acc[HEAD_DIM:HEAD_DIM + 1]
        pair = jnp.concatenate([ot[:, :tq], ot[:, tq:]], axis=0)
        o_ref[0, :, h * HEAD_DIM:(h + 2) * HEAD_DIM] = pair.T.astype(BF16)


def _attention(qt, k, vt, tq, score_bound):
    b, _, l = qt.shape
    nc, tk = vt.shape[1], vt.shape[4]

    def call(bounded):
        return pl.pallas_call(
            functools.partial(_attn_kernel, bounded=bounded),
            grid=(b, l // tq),
            in_specs=[pl.BlockSpec((1, ATTN_W, tq), lambda i, j: (i, 0, j)),
                      pl.BlockSpec((1, l, KV_W), lambda i, j: (i, 0, 0)),
                      pl.BlockSpec((1, nc, N_KV_HEADS, PV_ROWS, tk), lambda i, j: (i, 0, 0, 0, 0))],
            out_specs=pl.BlockSpec((1, tq, ATTN_W), lambda i, j: (i, j, 0)),
            out_shape=jax.ShapeDtypeStruct((b, l, ATTN_W), BF16),
            scratch_shapes=[pltpu.VMEM((N_KV_HEADS, KV_W, KV_GROUP * tq), BF16),
                            pltpu.VMEM((N_KV_HEADS, PV_ROWS, KV_GROUP * tq), F32),
                            pltpu.VMEM((N_KV_HEADS, 1, KV_GROUP * tq), F32)],
            compiler_params=_cparams("parallel", "parallel"),
            name="attention" if bounded else "attention_running_max",
        )

    return lax.cond(score_bound <= SAFE_SCORE, call(True), call(False), qt, k, vt)


def _complex_powers(lr, li, n):
    pr, pi = jnp.ones_like(lr)[None], jnp.zeros_like(li)[None]
    cr, ci = lr, li
    while pr.shape[0] < n:
        pr, pi = (jnp.concatenate([pr, pr * cr - pi * ci]), jnp.concatenate([pi, pr * ci + pi * cr]))
        cr, ci = cr * cr - ci * ci, 2.0 * cr * ci
    return pr[:n], pi[:n]


def _toeplitz_kernel(k_ref, o_ref):
    n, hch, t = k_ref.shape[1], k_ref.shape[2], k_ref.shape[3] // 2
    for i in range(n):
        for h in range(hch):
            rows = jnp.broadcast_to(k_ref[0, i, h:h + 1, :], (t, 2 * t))
            rolled = pltpu.roll(rows, t + 1, 1, stride=1, stride_axis=0)
            o_ref[0, i * t:(i + 1) * t, h * t:(h + 1) * t] = rolled[:, :t].astype(BF16)


def _toeplitz(k_cat):
    g, hin, hout, t2 = k_cat.shape
    t = t2 // 2
    n = _pick(hin, 4)
    return pl.pallas_call(
        _toeplitz_kernel,
        grid=(g, hin // n),
        in_specs=[pl.BlockSpec((1, n, hout, t2), lambda i, j: (i, j, 0, 0))],
        out_specs=pl.BlockSpec((1, n * t, hout * t), lambda i, j: (i, j, 0)),
        out_shape=jax.ShapeDtypeStruct((g, hin * t, hout * t), BF16),
        compiler_params=_cparams("parallel", "parallel"),
        name="s5_toeplitz",
    )(k_cat)


def _s5_operators(a_re, a_im, log_dt, b_re, b_im, c_re, c_im, t):
    hi = lax.Precision.HIGHEST
    a_re, a_im, log_dt = a_re.astype(F32), a_im.astype(F32), log_dt.astype(F32)
    b_re, b_im, c_re, c_im = (z.astype(F32) for z in (b_re, b_im, c_re, c_im))
    dt = jnp.exp(log_dt)[..., None]
    mag = jnp.exp(a_re * dt)
    lam_re, lam_im = mag * jnp.cos(a_im * dt), mag * jnp.sin(a_im * dt)
    num_re = lam_re - 1.0
    den = a_re * a_re + a_im * a_im
    coef_re = (num_re * a_re + lam_im * a_im) / den
    coef_im = (lam_im * a_re - num_re * a_im) / den
    pw_re, pw_im = _complex_powers(lam_re, lam_im, t + 1)
    e_re = pw_re * coef_re - pw_im * coef_im
    e_im = pw_re * coef_im + pw_im * coef_re
    g, p, hch = b_re.shape

    cb_re = c_re[:, :, :, None] * b_re[:, None, :, :] - c_im[:, :, :, None] * b_im[:, None, :, :]
    cb_im = c_re[:, :, :, None] * b_im[:, None, :, :] + c_im[:, :, :, None] * b_re[:, None, :, :]
    kern = (jnp.einsum('tdgp,ghpk->dgthk', e_re[:t], cb_re, precision=hi)
            - jnp.einsum('tdgp,ghpk->dgthk', e_im[:t], cb_im, precision=hi))
    k_f, k_b = kern[0], kern[1]
    k_cat = jnp.concatenate([k_b[:, :0:-1], (k_f[:, :1] + k_b[:, :1]), k_f[:, 1:]], axis=1)
    k_cat = jnp.pad(k_cat.transpose(0, 3, 2, 1), ((0, 0), (0, 0), (0, 0), (0, 1)))
    toep = _toeplitz(k_cat)

    def times_b(er, ei):
        re = er[:, :, :, None] * b_re[None] - ei[:, :, :, None] * b_im[None]
        im = er[:, :, :, None] * b_im[None] + ei[:, :, :, None] * b_re[None]
        f = lambda z: z.transpose(1, 3, 0, 2).reshape(g, hch * t, p)
        return f(re), f(im)
    sf_re, sf_im = times_b(e_re[:t, 0][::-1], e_im[:t, 0][::-1])
    sb_re, sb_im = times_b(e_re[:t, 1], e_im[:t, 1])
    w_s = jnp.concatenate([sf_re, sf_im, sb_re, sb_im], axis=2)

    def c_times(pr, pi):
        re = c_re[None] * pr[:, :, None, :] - c_im[None] * pi[:, :, None, :]
        im = c_re[None] * pi[:, :, None, :] + c_im[None] * pr[:, :, None, :]
        f = lambda z: z.transpose(1, 3, 2, 0).reshape(g, p, hch * t)
        return f(re), f(-im)
    of_re, of_im = c_times(pw_re[1:t + 1, 0], pw_im[1:t + 1, 0])
    ob_re, ob_im = c_times(pw_re[1:t + 1, 1][::-1], pw_im[1:t + 1, 1][::-1])
    w_o = jnp.concatenate([of_re, of_im, ob_re, ob_im], axis=1)

    lt_re, lt_im = pw_re[t], pw_im[t]
    a1 = jnp.concatenate([lt_re[0], lt_re[0], lt_re[1], lt_re[1]], axis=-1)[:, None, :]
    a2 = jnp.concatenate([-lt_im[0], lt_im[0], -lt_im[1], lt_im[1]], axis=-1)[:, None, :]
    return toep.astype(BF16), w_s.astype(BF16), w_o.astype(BF16), a1, a2


def _s5_chunk_rows(u_ref):
    b, hch, c, t = u_ref.shape
    return jnp.concatenate([u_ref[:, h].reshape(b * c, t) for h in range(hch)], axis=1)


def _s5_state_kernel(u_ref, ws_ref, s_ref):
    s_ref[0] = jnp.dot(_s5_chunk_rows(u_ref), ws_ref[0], preferred_element_type=F32)


def _s5_state(ut, w_s):
    b, ch, c, t = ut.shape
    g, k, n = w_s.shape
    return pl.pallas_call(
        _s5_state_kernel,
        grid=(g,),
        in_specs=[pl.BlockSpec((b, ch // g, c, t), lambda i: (0, i, 0, 0)),
                  pl.BlockSpec((1, k, n), lambda i: (i, 0, 0))],
        out_specs=pl.BlockSpec((1, b * c, n), lambda i: (i, 0, 0)),
        out_shape=jax.ShapeDtypeStruct((g, b * c, n), F32),
        compiler_params=_cparams("parallel"),
        name="s5_state",
    )(ut, w_s)


def _s5_scan_kernel(s_ref, a1_ref, a2_ref, h_ref, *, nb):
    gb, rows, _ = s_ref.shape
    nchunk = rows // nb
    half = 2 * SSM_STATE
    a1, a2 = a1_ref[...], a2_ref[...]
    a1f, a1b, a2f, a2b = a1[:, :, :half], a1[:, :, half:], a2[:, :, :half], a2[:, :, half:]

    def swap(z):
        return pltpu.roll(z.reshape(gb * nb, half), SSM_STATE, 1).reshape(gb, nb, half)

    def body(j, carry):
        hf, hb = carry
        rf = pl.ds(pl.multiple_of(j * nb, nb), nb)
        rb = pl.ds(pl.multiple_of((nchunk - 1 - j) * nb, nb), nb)
        h_ref[:, rf, :half] = hf
        h_ref[:, rb, half:] = hb
        hf = a1f * hf + a2f * swap(hf) + s_ref[:, rf, :half]
        hb = a1b * hb + a2b * swap(hb) + s_ref[:, rb, half:]
        return hf, hb

    zero = jnp.zeros((gb, nb, half), F32)
    lax.fori_loop(0, nchunk, body, (zero, zero))


def _s5_scan(s, a1, a2, nb, gb):
    g, r, n = s.shape
    blk = pl.BlockSpec((gb, r, n), lambda i: (i, 0, 0))
    par = pl.BlockSpec((gb, 1, n), lambda i: (i, 0, 0))
    return pl.pallas_call(
        functools.partial(_s5_scan_kernel, nb=nb),
        grid=(g // gb,),
        in_specs=[blk, par, par],
        out_specs=blk,
        out_shape=jax.ShapeDtypeStruct((g, r, n), F32),
        compiler_params=_cparams("parallel"),
        name="s5_scan",
    )(s, a1, a2)


def _s5_out_kernel(u_ref, h_ref, toep_ref, wo_ref, y_ref):
    b, hch, c, t = u_ref.shape
    y = jnp.dot(_s5_chunk_rows(u_ref), toep_ref[0], preferred_element_type=F32)
    y = y + jnp.dot(h_ref[0].astype(BF16), wo_ref[0], preferred_element_type=F32)
    for h in range(hch):
        y_ref[:, h] = y[:, h * t:(h + 1) * t].reshape(b, c, t)


def _s5_out(ut, hin, toep, w_o):
    b, ch, c, t = ut.shape
    g, n, k = w_o.shape
    blk = pl.BlockSpec((b, ch // g, c, t), lambda i: (0, i, 0, 0))
    return pl.pallas_call(
        _s5_out_kernel,
        grid=(g,),
        in_specs=[blk, pl.BlockSpec((1, b * c, n), lambda i: (i, 0, 0)),
                  pl.BlockSpec((1, k, k), lambda i: (i, 0, 0)),
                  pl.BlockSpec((1, n, k), lambda i: (i, 0, 0))],
        out_specs=blk,
        out_shape=jax.ShapeDtypeStruct((b, ch, c, t), F32),
        compiler_params=_cparams("parallel"),
        name="s5_out",
    )(ut, hin, toep, w_o)


def _s5_scan_branch(ut, ops):
    toep, w_s, w_o, a1, a2 = ops
    b, ch, l = ut.shape
    t = toep.shape[1] // SSM_GROUP_CH
    c = l // t
    g, n = w_s.shape[0], w_s.shape[2]
    ut = ut.reshape(b, ch, c, t)
    s = _s5_state(ut, w_s)
    s = s.reshape(g, b, c, n).transpose(0, 2, 1, 3).reshape(g, c * b, n)
    hin = _s5_scan(s, a1, a2, b, 4)
    hin = hin.reshape(g, c, b, n).transpose(0, 2, 1, 3).reshape(g, b * c, n)
    return _s5_out(ut, hin, toep, w_o).reshape(b, ch, l)


def _mem_kv_kernel(m_ref, g_ref, w_ref, o_ref):
    h = _rms(m_ref[0], g_ref[...]).astype(BF16)
    o_ref[0] = jnp.dot(h, w_ref[...], preferred_element_type=F32).astype(BF16)


def _mem_kv(mem, gain, w):
    b, m, d = mem.shape
    n = w.shape[1]
    return pl.pallas_call(
        _mem_kv_kernel,
        grid=(b,),
        in_specs=[pl.BlockSpec((1, m, d), lambda i: (i, 0, 0)), _const_spec((1, d)), _const_spec((d, n))],
        out_specs=pl.BlockSpec((1, m, n), lambda i: (i, 0, 0)),
        out_shape=jax.ShapeDtypeStruct((b, m, n), BF16),
        compiler_params=_cparams("parallel"),
        name="mem_kv",
    )(mem, gain, w)


def _merge_kernel(x_ref, a_ref, y_ref, u_ref, qc_ref, gate_ref, mkv_ref, dskip_ref, wglu_ref,
                  pa_ref, ps_ref, pc_ref, wout_ref, o_ref, *, sub):
    d = x_ref.shape[2]
    mv = mkv_ref[0, :, CROSS_W:]
    for r0 in range(0, x_ref.shape[1], sub):
        rows = slice(r0, r0 + sub)
        heads = []
        for h in range(N_CROSS_HEADS):
            sl = slice(h * HEAD_DIM, (h + 1) * HEAD_DIM)
            pair = slice((h // 2) * LANES, (h // 2 + 1) * LANES)
            s = lax.dot_general(_head_lanes(qc_ref[0, rows, pair], h), _head_lanes(mkv_ref[0, :, pair], h),
                                (((1,), (1,)), ((), ())), preferred_element_type=F32)
            p = jnp.exp(s - jnp.max(s, axis=-1, keepdims=True))
            den = jnp.sum(p, axis=-1, keepdims=True)
            o = jnp.dot(p.astype(BF16), mv, preferred_element_type=F32)
            heads.append(o[:, sl] / den)
        cross = jnp.concatenate(heads, axis=1).astype(BF16)
        y = y_ref[0, :, rows].T + dskip_ref[...] * u_ref[0, rows, :].astype(F32)
        y = jax.nn.gelu(y)
        y = y * jax.nn.sigmoid(jnp.dot(y.astype(BF16), wglu_ref[...], preferred_element_type=F32))
        gate = lambda j: gate_ref[0, rows, j * d:(j + 1) * d].astype(F32)
        m = gate(0) * jnp.dot(a_ref[0, rows, :], pa_ref[...], preferred_element_type=F32)
        m = m + gate(1) * jnp.dot(y.astype(BF16), ps_ref[...], preferred_element_type=F32)
        m = m + gate(2) * jnp.dot(cross, pc_ref[...], preferred_element_type=F32)
        o_ref[0, rows, :] = x_ref[0, rows, :] + jnp.dot(m.astype(BF16), wout_ref[...], preferred_element_type=F32)


def _merge(x, attn, yscan, u, qc, gates, mkv, d_skip, w_glu, p_attn, p_ssm, p_cross, w_out, tl):
    b, l, d = x.shape
    row = lambda width: pl.BlockSpec((1, tl, width), lambda i, j: (i, j, 0))
    n_mem = mkv.shape[1]
    return pl.pallas_call(
        functools.partial(_merge_kernel, sub=tl),
        grid=(b, l // tl),
        in_specs=[row(d), row(ATTN_W), pl.BlockSpec((1, SSM_W, tl), lambda i, j: (i, 0, j)),
                  row(SSM_W), row(CROSS_W), row(N_BRANCHES * d),
                  pl.BlockSpec((1, n_mem, 2 * CROSS_W), lambda i, j: (i, 0, 0)),
                  _const_spec((1, SSM_W)), _const_spec((SSM_W, SSM_W)),
                  _const_spec((ATTN_W, d)), _const_spec((SSM_W, d)), _const_spec((CROSS_W, d)),
                  _const_spec((d, d))],
        out_specs=row(d),
        out_shape=jax.ShapeDtypeStruct((b, l, d), F32),
        compiler_params=_cparams("parallel", "parallel"),
        name="merge",
    )(x, attn, yscan, u, qc, gates, mkv, d_skip, w_glu, p_attn, p_ssm, p_cross, w_out)


MXU_TILE = 256


def _ffn_chunks(dff, tiles_per_chunk=11):
    step = tiles_per_chunk * MXU_TILE
    return [min(step, dff - c0) for c0 in range(0, dff, step)]


def _ffn_kernel(xp_ref, x_ref, xn_ref, g_ref, wup_ref, cw_ref, cb_ref, wdn_ref, gf_ref, o_ref,
                h_scr, gate_scr, *, final_norm):
    j = pl.program_id(1)
    nj = pl.num_programs(1)
    tm = x_ref.shape[1]
    dff = wdn_ref.shape[0]
    gain = g_ref[...]
    x = x_ref[0]
    h_scr[0:HALO, :] = jnp.where(j == 0, 0.0, _rms(xp_ref[0], gain)).astype(BF16)
    h_scr[HALO:HALO + tm, :] = _rms(x, gain).astype(BF16)
    h_scr[HALO + tm:, :] = jnp.where(j == nj - 1, 0.0, _rms(xn_ref[0], gain)).astype(BF16)
    y = x
    c0 = 0
    for wc in _ffn_chunks(dff):
        gs = gate_scr.at[:, c0:c0 + wc]
        gs[...] = jnp.dot(h_scr[...], wup_ref[:, c0:c0 + wc], preferred_element_type=F32)
        val = jnp.dot(h_scr[HALO:HALO + tm, :], wup_ref[:, dff + c0:dff + c0 + wc], preferred_element_type=F32)
        cw = cw_ref[:, c0:c0 + wc]
        gc = (gs[HALO - 1:HALO - 1 + tm, :] * cw[0:1, :] + gs[HALO:HALO + tm, :] * cw[1:2, :]
              + gs[HALO + 1:HALO + 1 + tm, :] * cw[2:3, :] + cb_ref[:, c0:c0 + wc])
        act = (jax.nn.gelu(gc) * val).astype(BF16)
        y = y + jnp.dot(act, wdn_ref[c0:c0 + wc, :], preferred_element_type=F32)
        c0 += wc
    if final_norm:
        y = _rms(y, gf_ref[...])
    o_ref[0] = y


def _ffn(x, gain, w_up, conv_w, conv_b, w_down, final_gain, final_norm, tm):
    b, l, d = x.shape
    dff = w_down.shape[0]
    r = tm // HALO
    nh = l // HALO
    return pl.pallas_call(
        functools.partial(_ffn_kernel, final_norm=final_norm),
        grid=(b, l // tm),
        in_specs=[pl.BlockSpec((1, HALO, d), lambda i, j: (i, jnp.maximum(j * r - 1, 0), 0)),
                  pl.BlockSpec((1, tm, d), lambda i, j: (i, j, 0)),
                  pl.BlockSpec((1, HALO, d), lambda i, j: (i, jnp.minimum((j + 1) * r, nh - 1), 0)),
                  _const_spec((1, d)), _const_spec((d, 2 * dff)), _const_spec((3, dff)),
                  _const_spec((1, dff)), _const_spec((dff, d)), _const_spec((1, d))],
        out_specs=pl.BlockSpec((1, tm, d), lambda i, j: (i, j, 0)),
        out_shape=jax.ShapeDtypeStruct((b, l, d), F32),
        scratch_shapes=[pltpu.VMEM((tm + 2 * HALO, d), BF16),
                        pltpu.VMEM((tm + 2 * HALO, dff), F32)],
        compiler_params=_cparams("parallel", "arbitrary"),
        name="ffn",
    )(x, x, x, gain, w_up, conv_w, conv_b, w_down, final_gain)


def _rope_tables(l):
    rows = l // GRID_W
    r = jnp.broadcast_to(jnp.arange(rows, dtype=F32)[:, None], (rows, GRID_W)).reshape(l)
    c = jnp.broadcast_to(jnp.arange(GRID_W, dtype=F32)[None, :], (rows, GRID_W)).reshape(l)
    freqs = ROPE_THETA ** (-jnp.arange(ROPE_PAIRS, dtype=F32) / ROPE_PAIRS)
    ang_r, ang_c = r[:, None] * freqs, c[:, None] * freqs
    ang = jnp.concatenate([ang_r, ang_r, ang_c, ang_c], axis=-1)
    cos, sin = jnp.cos(ang), jnp.sin(ang)
    first = (jnp.arange(HEAD_DIM) % (2 * ROPE_PAIRS)) < ROPE_PAIRS
    sin_up = jnp.where(first, -sin, 0.0)
    sin_dn = jnp.where(first, 0.0, sin)
    two = lambda a: jnp.concatenate([a, a], axis=1)
    return two(cos), two(sin_up), two(sin_dn)


def _prepare_layer(p, l):
    bf = lambda a: a.astype(BF16)
    row = lambda a: a.astype(F32).reshape(1, -1)
    max_abs = lambda a: jnp.max(jnp.abs(a.astype(F32)))
    score_bound = 1.02 * QK_SCALE * HEAD_DIM * max_abs(p['q_norm'][l]) * max_abs(p['k_norm'][l])
    return dict(
        norm_mix=row(p['norm_mix'][l]), w_in=bf(p['w_in'][l]),
        q_gain=row(jnp.tile(p['q_norm'][l], N_HEADS)), k_gain=row(jnp.tile(p['k_norm'][l], N_KV_HEADS)),
        score_bound=score_bound,
        s5=_s5_operators(p['ssm_a_re'][l], p['ssm_a_im'][l], p['ssm_log_dt'][l], p['ssm_b_re'][l],
                         p['ssm_b_im'][l], p['ssm_c_re'][l], p['ssm_c_im'][l], S5_CHUNK),
        d_skip=row(p['ssm_d'][l]), w_glu=bf(p['ssm_glu'][l]),
        mem_norm=row(p['mem_norm'][l]), w_mem_kv=bf(p['w_mem_kv'][l]),
        p_attn=bf(p['p_attn'][l]), p_ssm=bf(p['p_ssm'][l]), p_cross=bf(p['p_cross'][l]), w_out=bf(p['w_out'][l]),
        norm_ffn=row(p['norm_ffn'][l]), w_up=bf(p['w_up'][l]), conv_w=p['conv_w'][l].astype(F32),
        conv_b=row(p['conv_b'][l]), w_down=bf(p['w_down'][l]),
    )


def _encode(x, mem, layers, final_gain):
    b, l, d = x.shape
    tables = _rope_tables(l)
    tk = _pick(l, 512)
    tq = _pick(l, 512)
    tm = _pick(l, 512)
    for li, w in enumerate(layers):
        qt, k, vt, u, ut, qc, gates = _in_proj(x, w['norm_mix'], w['w_in'], tables, w['q_gain'], w['k_gain'], tk)
        attn = _attention(qt, k, vt, tq, w['score_bound'])
        yscan = _s5_scan_branch(ut, w['s5'])
        mkv = _mem_kv(mem, w['mem_norm'], w['w_mem_kv'])
        x = _merge(x, attn, yscan, u, qc, gates, mkv, w['d_skip'], w['w_glu'], w['p_attn'], w['p_ssm'],
                   w['p_cross'], w['w_out'], tm)
        x = _ffn(x, w['norm_ffn'], w['w_up'], w['conv_w'], w['conv_b'], w['w_down'], final_gain,
                 li == len(layers) - 1, tm)
    return x


def kernel(x_prompt, x_sample, mem_prompt, mem_sample, norm_mix, w_in, q_norm, k_norm, ssm_a_re, ssm_a_im, ssm_log_dt, ssm_b_re, ssm_b_im, ssm_c_re, ssm_c_im, ssm_d, ssm_glu, mem_norm, w_mem_kv, p_attn, p_ssm, p_cross, w_out, norm_ffn, w_up, conv_w, conv_b, w_down, norm_final):
    p = dict(norm_mix=norm_mix, w_in=w_in, q_norm=q_norm, k_norm=k_norm, ssm_a_re=ssm_a_re, ssm_a_im=ssm_a_im,
             ssm_log_dt=ssm_log_dt, ssm_b_re=ssm_b_re, ssm_b_im=ssm_b_im, ssm_c_re=ssm_c_re, ssm_c_im=ssm_c_im,
             ssm_d=ssm_d, ssm_glu=ssm_glu, mem_norm=mem_norm, w_mem_kv=w_mem_kv, p_attn=p_attn, p_ssm=p_ssm,
             p_cross=p_cross, w_out=w_out, norm_ffn=norm_ffn, w_up=w_up, conv_w=conv_w, conv_b=conv_b,
             w_down=w_down)
    layers = [_prepare_layer(p, l) for l in range(norm_mix.shape[0])]
    final_gain = norm_final.astype(F32).reshape(1, -1)
    y_prompt = _encode(x_prompt, mem_prompt, layers, final_gain)
    y_sample = _encode(x_sample, mem_sample, layers, final_gain)
    return (y_prompt, y_sample)
```

```python
import functools
import math

import jax
import jax.numpy as jnp
from jax import lax
from jax.experimental import pallas as pl
from jax.experimental.pallas import tpu as pltpu

HEAD_DIM = 64
N_HEADS = 8
N_KV_HEADS = 2
KV_GROUP = N_HEADS // N_KV_HEADS
ATTN_W = N_HEADS * HEAD_DIM
KV_W = N_KV_HEADS * HEAD_DIM
SSM_GROUP_CH = 16
SSM_W = 256
SSM_GROUPS = SSM_W // SSM_GROUP_CH
SSM_STATE = 64
N_CROSS_HEADS = 4
CROSS_W = N_CROSS_HEADS * HEAD_DIM
N_BRANCHES = 3
GRID_W = 64
ROPE_THETA = 10000.0
ROPE_PAIRS = HEAD_DIM // 4
EPS = 1e-6

LOG2E = 1.4426950408889634
QK_SCALE = HEAD_DIM ** -0.5 * LOG2E
SAFE_SCORE = 64.0
LANES = 128
S5_CHUNK = LANES
PV_ROWS = 2 * HEAD_DIM
HALO = 16
VMEM_LIMIT = 56 * 1024 * 1024

F32 = jnp.float32
BF16 = jnp.bfloat16


def _cparams(*sem):
    return pltpu.CompilerParams(dimension_semantics=sem, vmem_limit_bytes=VMEM_LIMIT)


def _const_spec(shape):
    nd = len(shape)
    return pl.BlockSpec(shape, lambda *_: (0,) * nd, pipeline_mode=pl.Buffered(1))


def _rms(x, gain):
    return x * lax.rsqrt(jnp.mean(x * x, axis=-1, keepdims=True) + EPS) * gain


def _pick(n, pref):
    t = min(n, pref)
    while n % t:
        t //= 2
    return t


def _head_norm_rope(x, gain, cos, sin_up, sin_dn, out_scale):
    w = x.shape[1]
    reps = w // LANES
    tile = lambda a: a if reps == 1 else jnp.concatenate([a] * reps, axis=1)
    xg = x * gain
    y = (xg * tile(cos)
         + pltpu.roll(xg, w - ROPE_PAIRS, 1) * tile(sin_up)
         + pltpu.roll(xg, ROPE_PAIRS, 1) * tile(sin_dn))
    lane = lax.broadcasted_iota(jnp.int32, (1, LANES), 1)
    low = lane < HEAD_DIM
    outs = []
    for p in range(reps):
        xp = x[:, p * LANES:(p + 1) * LANES]
        sq = xp * xp
        ss_all = jnp.sum(sq, axis=-1, keepdims=True)
        ss_lo = jnp.sum(jnp.where(low, sq, 0.0), axis=-1, keepdims=True)
        r_lo = lax.rsqrt(ss_lo * (1.0 / HEAD_DIM) + EPS) * out_scale
        r_hi = lax.rsqrt((ss_all - ss_lo) * (1.0 / HEAD_DIM) + EPS) * out_scale
        outs.append(y[:, p * LANES:(p + 1) * LANES] * jnp.where(low, r_lo, r_hi))
    return outs[0] if reps == 1 else jnp.concatenate(outs, axis=1)


def _in_proj_kernel(x_ref, g_ref, w_ref, cos_ref, sup_ref, sdn_ref, qg_ref, kg_ref,
                    qt_ref, k_ref, vt_ref, u_ref, ut_ref, qc_ref, gate_ref, h_scr):
    h_scr[...] = _rms(x_ref[0], g_ref[...]).astype(BF16)
    z = jnp.dot(h_scr[...], w_ref[...], preferred_element_type=F32)

    def seg(a, b):
        return z[:, a:b]

    cos, sup, sdn = cos_ref[...], sup_ref[...], sdn_ref[...]
    o = 0
    q = seg(o, o + ATTN_W)
    qt_ref[0] = _head_norm_rope(q, qg_ref[...], cos, sup, sdn, QK_SCALE).T.astype(BF16)
    o += ATTN_W
    kv = seg(o, o + 2 * KV_W)
    k_ref[0] = _head_norm_rope(kv[:, :KV_W], kg_ref[...], cos, sup, sdn, 1.0).astype(BF16)
    vt = kv[:, KV_W:].T.astype(BF16)
    ones = jnp.ones((PV_ROWS - HEAD_DIM, vt.shape[1]), BF16)
    for g in range(N_KV_HEADS):
        vt_ref[0, 0, g, :HEAD_DIM, :] = vt[g * HEAD_DIM:(g + 1) * HEAD_DIM, :]
        vt_ref[0, 0, g, HEAD_DIM:, :] = ones
    o += 2 * KV_W
    u = seg(o, o + SSM_W)
    u_ref[0] = u.astype(BF16)
    ut_ref[0] = u.T.astype(BF16)
    o += SSM_W
    qc_ref[0] = (seg(o, o + CROSS_W) * HEAD_DIM ** -0.5).T.astype(BF16)
    o += CROSS_W
    d = x_ref.shape[2]
    for j in range(N_BRANCHES):
        gate_ref[0, :, j * d:(j + 1) * d] = jax.nn.sigmoid(seg(o + j * d, o + (j + 1) * d)).astype(BF16)


def _in_proj(x, gain, w, tables, q_gain, k_gain, tk):
    b, l, d = x.shape
    n_in = w.shape[1]
    nj = l // tk
    row = lambda width: pl.BlockSpec((1, tk, width), lambda i, j: (i, j, 0))
    tab = pl.BlockSpec((tk, LANES), lambda i, j: (j, 0))
    out_shape = (
        jax.ShapeDtypeStruct((b, ATTN_W, l), BF16),
        jax.ShapeDtypeStruct((b, l, KV_W), BF16),
        jax.ShapeDtypeStruct((b, nj, N_KV_HEADS, PV_ROWS, tk), BF16),
        jax.ShapeDtypeStruct((b, l, SSM_W), BF16),
        jax.ShapeDtypeStruct((b, SSM_W, l), BF16),
        jax.ShapeDtypeStruct((b, CROSS_W, l), BF16),
        jax.ShapeDtypeStruct((b, l, N_BRANCHES * d), BF16),
    )
    col = lambda height: pl.BlockSpec((1, height, tk), lambda i, j: (i, 0, j))
    return pl.pallas_call(
        _in_proj_kernel,
        grid=(b, nj),
        in_specs=[row(d), _const_spec((1, d)), _const_spec((d, n_in)), tab, tab, tab,
                  _const_spec((1, ATTN_W)), _const_spec((1, KV_W))],
        out_specs=(col(ATTN_W), row(KV_W),
                   pl.BlockSpec((1, 1, N_KV_HEADS, PV_ROWS, tk), lambda i, j: (i, j, 0, 0, 0)),
                   row(SSM_W), col(SSM_W), col(CROSS_W), row(N_BRANCHES * d)),
        out_shape=out_shape,
        scratch_shapes=[pltpu.VMEM((tk, d), BF16)],
        compiler_params=_cparams("parallel", "parallel"),
        name="in_proj",
    )(x, gain, w, *tables, q_gain, k_gain)


def _head_lanes(pair, h):
    return pair[:, (h % 2) * HEAD_DIM:(h % 2 + 1) * HEAD_DIM]


def _attn_kernel(qt_ref, k_ref, vt_ref, o_ref, qs_ref, acc_ref, m_ref, *, bounded):
    tq = qt_ref.shape[2]
    nc, tk = vt_ref.shape[1], vt_ref.shape[4]
    qs_ref[...] = jnp.zeros(qs_ref.shape, BF16)
    for h in range(N_HEADS):
        g, r = divmod(h, KV_GROUP)
        qs_ref[g, g * HEAD_DIM:(g + 1) * HEAD_DIM, r * tq:(r + 1) * tq] = qt_ref[0, h * HEAD_DIM:(h + 1) * HEAD_DIM, :]
    acc_ref[...] = jnp.zeros(acc_ref.shape, F32)
    if not bounded:
        m_ref[...] = jnp.full(m_ref.shape, -jnp.inf, F32)

    def body(c, carry):
        kc = k_ref[0, pl.ds(pl.multiple_of(c * tk, tk), tk), :]
        for g in range(N_KV_HEADS):
            st = jnp.dot(kc, qs_ref[g], preferred_element_type=F32)
            if bounded:
                pt = jnp.exp2(st).astype(BF16)
                acc_ref[g] += jnp.dot(vt_ref[0, c, g], pt, preferred_element_type=F32)
            else:
                m_old = m_ref[g]
                m_new = jnp.maximum(m_old, jnp.max(st, axis=0, keepdims=True))
                pt = jnp.exp2(st - m_new).astype(BF16)
                acc_ref[g] = (jnp.exp2(m_old - m_new) * acc_ref[g]
                              + jnp.dot(vt_ref[0, c, g], pt, preferred_element_type=F32))
                m_ref[g] = m_new
        return carry

    lax.fori_loop(0, nc, body, 0, unroll=4)
    for h in range(0, N_HEADS, 2):
        g, r = divmod(h, KV_GROUP)
        acc = acc_ref[g, :, r * tq:(r + 2) * tq]
        ot = acc[:HEAD_DIM] / acc[HEAD_DIM:HEAD_DIM + 1]
        pair = jnp.concatenate([ot[:, :tq], ot[:, tq:]], axis=0)
        o_ref[0, :, h * HEAD_DIM:(h + 2) * HEAD_DIM] = pair.T.astype(BF16)


def _attention(qt, k, vt, tq, score_bound):
    b, _, l = qt.shape
    nc, tk = vt.shape[1], vt.shape[4]

    def call(bounded):
        return pl.pallas_call(
            functools.partial(_attn_kernel, bounded=bounded),
            grid=(b, l // tq),
            in_specs=[pl.BlockSpec((1, ATTN_W, tq), lambda i, j: (i, 0, j)),
                      pl.BlockSpec((1, l, KV_W), lambda i, j: (i, 0, 0)),
                      pl.BlockSpec((1, nc, N_KV_HEADS, PV_ROWS, tk), lambda i, j: (i, 0, 0, 0, 0))],
            out_specs=pl.BlockSpec((1, tq, ATTN_W), lambda i, j: (i, j, 0)),
            out_shape=jax.ShapeDtypeStruct((b, l, ATTN_W), BF16),
            scratch_shapes=[pltpu.VMEM((N_KV_HEADS, KV_W, KV_GROUP * tq), BF16),
                            pltpu.VMEM((N_KV_HEADS, PV_ROWS, KV_GROUP * tq), F32),
                            pltpu.VMEM((N_KV_HEADS, 1, KV_GROUP * tq), F32)],
            compiler_params=_cparams("parallel", "parallel"),
            name="attention" if bounded else "attention_running_max",
        )

    return lax.cond(score_bound <= SAFE_SCORE, call(True), call(False), qt, k, vt)


def _complex_powers(lr, li, n):
    pr, pi = jnp.ones_like(lr)[None], jnp.zeros_like(li)[None]
    cr, ci = lr, li
    while pr.shape[0] < n:
        pr, pi = (jnp.concatenate([pr, pr * cr - pi * ci]), jnp.concatenate([pi, pr * ci + pi * cr]))
        cr, ci = cr * cr - ci * ci, 2.0 * cr * ci
    return pr[:n], pi[:n]


def _toeplitz_kernel(k_ref, o_ref):
    n, hch, t = k_ref.shape[1], k_ref.shape[2], k_ref.shape[3] // 2
    for i in range(n):
        for h in range(hch):
            rows = jnp.broadcast_to(k_ref[0, i, h:h + 1, :], (t, 2 * t))
            rolled = pltpu.roll(rows, t + 1, 1, stride=1, stride_axis=0)
            o_ref[0, i * t:(i + 1) * t, h * t:(h + 1) * t] = rolled[:, :t].astype(BF16)


def _toeplitz(k_cat):
    g, hin, hout, t2 = k_cat.shape
    t = t2 // 2
    n = _pick(hin, 4)
    return pl.pallas_call(
        _toeplitz_kernel,
        grid=(g, hin // n),
        in_specs=[pl.BlockSpec((1, n, hout, t2), lambda i, j: (i, j, 0, 0))],
        out_specs=pl.BlockSpec((1, n * t, hout * t), lambda i, j: (i, j, 0)),
        out_shape=jax.ShapeDtypeStruct((g, hin * t, hout * t), BF16),
        compiler_params=_cparams("parallel", "parallel"),
        name="s5_toeplitz",
    )(k_cat)


def _s5_operators(a_re, a_im, log_dt, b_re, b_im, c_re, c_im, t):
    hi = lax.Precision.HIGHEST
    a_re, a_im, log_dt = a_re.astype(F32), a_im.astype(F32), log_dt.astype(F32)
    b_re, b_im, c_re, c_im = (z.astype(F32) for z in (b_re, b_im, c_re, c_im))
    dt = jnp.exp(log_dt)[..., None]
    mag = jnp.exp(a_re * dt)
    lam_re, lam_im = mag * jnp.cos(a_im * dt), mag * jnp.sin(a_im * dt)
    num_re = lam_re - 1.0
    den = a_re * a_re + a_im * a_im
    coef_re = (num_re * a_re + lam_im * a_im) / den
    coef_im = (lam_im * a_re - num_re * a_im) / den
    pw_re, pw_im = _complex_powers(lam_re, lam_im, t + 1)
    e_re = pw_re * coef_re - pw_im * coef_im
    e_im = pw_re * coef_im + pw_im * coef_re
    g, p, hch = b_re.shape

    cb_re = c_re[:, :, :, None] * b_re[:, None, :, :] - c_im[:, :, :, None] * b_im[:, None, :, :]
    cb_im = c_re[:, :, :, None] * b_im[:, None, :, :] + c_im[:, :, :, None] * b_re[:, None, :, :]
    kern = (jnp.einsum('tdgp,ghpk->dgthk', e_re[:t], cb_re, precision=hi)
            - jnp.einsum('tdgp,ghpk->dgthk', e_im[:t], cb_im, precision=hi))
    k_f, k_b = kern[0], kern[1]
    k_cat = jnp.concatenate([k_b[:, :0:-1], (k_f[:, :1] + k_b[:, :1]), k_f[:, 1:]], axis=1)
    k_cat = jnp.pad(k_cat.transpose(0, 3, 2, 1), ((0, 0), (0, 0), (0, 0), (0, 1)))
    toep = _toeplitz(k_cat)

    bt_re, bt_im = b_re.transpose(0, 2, 1)[:, :, None, :], b_im.transpose(0, 2, 1)[:, :, None, :]

    def times_b(er, ei):
        er, ei = er.transpose(1, 0, 2)[:, None], ei.transpose(1, 0, 2)[:, None]
        re = er * bt_re - ei * bt_im
        im = er * bt_im + ei * bt_re
        f = lambda z: z.reshape(g, hch * t, p)
        return f(re), f(im)
    sf_re, sf_im = times_b(e_re[:t, 0][::-1], e_im[:t, 0][::-1])
    sb_re, sb_im = times_b(e_re[:t, 1], e_im[:t, 1])
    w_s = jnp.concatenate([sf_re, sf_im, sb_re, sb_im], axis=2)

    ct_re, ct_im = c_re.transpose(0, 2, 1)[:, :, :, None], c_im.transpose(0, 2, 1)[:, :, :, None]

    def c_times(pr, pi):
        pr, pi = pr.transpose(1, 2, 0)[:, :, None, :], pi.transpose(1, 2, 0)[:, :, None, :]
        re = ct_re * pr - ct_im * pi
        im = ct_re * pi + ct_im * pr
        f = lambda z: z.reshape(g, p, hch * t)
        return f(re), f(-im)
    of_re, of_im = c_times(pw_re[1:t + 1, 0], pw_im[1:t + 1, 0])
    ob_re, ob_im = c_times(pw_re[1:t + 1, 1][::-1], pw_im[1:t + 1, 1][::-1])
    w_o = jnp.concatenate([of_re, of_im, ob_re, ob_im], axis=1)

    lt_re, lt_im = pw_re[t], pw_im[t]
    a1 = jnp.concatenate([lt_re[0], lt_re[0], lt_re[1], lt_re[1]], axis=-1)[:, None, :]
    a2 = jnp.concatenate([-lt_im[0], lt_im[0], -lt_im[1], lt_im[1]], axis=-1)[:, None, :]
    return toep.astype(BF16), w_s.astype(BF16), w_o.astype(BF16), a1, a2


def _s5_chunk_rows(u_ref):
    b, hch, c, t = u_ref.shape
    return jnp.concatenate([u_ref[:, h].reshape(b * c, t) for h in range(hch)], axis=1)


def _s5_state_kernel(u_ref, ws_ref, s_ref):
    s_ref[0] = jnp.dot(_s5_chunk_rows(u_ref), ws_ref[0], preferred_element_type=F32)


def _s5_state(ut, w_s):
    b, ch, c, t = ut.shape
    g, k, n = w_s.shape
    return pl.pallas_call(
        _s5_state_kernel,
        grid=(g,),
        in_specs=[pl.BlockSpec((b, ch // g, c, t), lambda i: (0, i, 0, 0)),
                  pl.BlockSpec((1, k, n), lambda i: (i, 0, 0))],
        out_specs=pl.BlockSpec((1, b * c, n), lambda i: (i, 0, 0)),
        out_shape=jax.ShapeDtypeStruct((g, b * c, n), F32),
        compiler_params=_cparams("parallel"),
        name="s5_state",
    )(ut, w_s)


def _s5_scan_kernel(s_ref, a1_ref, a2_ref, h_ref, *, nb):
    gb, rows, _ = s_ref.shape
    nchunk = rows // nb
    half = 2 * SSM_STATE
    a1, a2 = a1_ref[...], a2_ref[...]
    a1f, a1b, a2f, a2b = a1[:, :, :half], a1[:, :, half:], a2[:, :, :half], a2[:, :, half:]

    def swap(z):
        return pltpu.roll(z.reshape(gb * nb, half), SSM_STATE, 1).reshape(gb, nb, half)

    def body(j, carry):
        hf, hb = carry
        rf = pl.ds(pl.multiple_of(j * nb, nb), nb)
        rb = pl.ds(pl.multiple_of((nchunk - 1 - j) * nb, nb), nb)
        h_ref[:, rf, :half] = hf
        h_ref[:, rb, half:] = hb
        hf = a1f * hf + a2f * swap(hf) + s_ref[:, rf, :half]
        hb = a1b * hb + a2b * swap(hb) + s_ref[:, rb, half:]
        return hf, hb

    zero = jnp.zeros((gb, nb, half), F32)
    lax.fori_loop(0, nchunk, body, (zero, zero))


def _s5_scan(s, a1, a2, nb, gb):
    g, r, n = s.shape
    blk = pl.BlockSpec((gb, r, n), lambda i: (i, 0, 0))
    par = pl.BlockSpec((gb, 1, n), lambda i: (i, 0, 0))
    return pl.pallas_call(
        functools.partial(_s5_scan_kernel, nb=nb),
        grid=(g // gb,),
        in_specs=[blk, par, par],
        out_specs=blk,
        out_shape=jax.ShapeDtypeStruct((g, r, n), F32),
        compiler_params=_cparams("parallel"),
        name="s5_scan",
    )(s, a1, a2)


def _s5_out_kernel(u_ref, h_ref, toep_ref, wo_ref, y_ref):
    b, hch, c, t = u_ref.shape
    y = jnp.dot(_s5_chunk_rows(u_ref), toep_ref[0], preferred_element_type=F32)
    y = y + jnp.dot(h_ref[0].astype(BF16), wo_ref[0], preferred_element_type=F32)
    for h in range(hch):
        y_ref[:, h] = y[:, h * t:(h + 1) * t].reshape(b, c, t)


def _s5_out(ut, hin, toep, w_o):
    b, ch, c, t = ut.shape
    g, n, k = w_o.shape
    blk = pl.BlockSpec((b, ch // g, c, t), lambda i: (0, i, 0, 0))
    return pl.pallas_call(
        _s5_out_kernel,
        grid=(g,),
        in_specs=[blk, pl.BlockSpec((1, b * c, n), lambda i: (i, 0, 0)),
                  pl.BlockSpec((1, k, k), lambda i: (i, 0, 0)),
                  pl.BlockSpec((1, n, k), lambda i: (i, 0, 0))],
        out_specs=blk,
        out_shape=jax.ShapeDtypeStruct((b, ch, c, t), F32),
        compiler_params=_cparams("parallel"),
        name="s5_out",
    )(ut, hin, toep, w_o)


def _s5_scan_branch(ut, ops):
    toep, w_s, w_o, a1, a2 = ops
    b, ch, l = ut.shape
    t = toep.shape[1] // SSM_GROUP_CH
    c = l // t
    g, n = w_s.shape[0], w_s.shape[2]
    ut = ut.reshape(b, ch, c, t)
    s = _s5_state(ut, w_s)
    s = s.reshape(g, b, c, n).transpose(0, 2, 1, 3).reshape(g, c * b, n)
    hin = _s5_scan(s, a1, a2, b, 4)
    hin = hin.reshape(g, c, b, n).transpose(0, 2, 1, 3).reshape(g, b * c, n)
    return _s5_out(ut, hin, toep, w_o).reshape(b, ch, l)


def _mem_kv_kernel(m_ref, g_ref, w_ref, mk_ref, mvt_ref):
    h = _rms(m_ref[0], g_ref[...]).astype(BF16)
    kv = jnp.dot(h, w_ref[...], preferred_element_type=F32)
    mk_ref[0] = kv[:, :CROSS_W].astype(BF16)
    vt = kv[:, CROSS_W:].T.astype(BF16)
    ones = jnp.ones((PV_ROWS - HEAD_DIM, vt.shape[1]), BF16)
    for hd in range(N_CROSS_HEADS):
        mvt_ref[0, hd, :HEAD_DIM, :] = vt[hd * HEAD_DIM:(hd + 1) * HEAD_DIM, :]
        mvt_ref[0, hd, HEAD_DIM:, :] = ones


def _mem_kv(mem, gain, w):
    b, m, d = mem.shape
    n = w.shape[1]
    return pl.pallas_call(
        _mem_kv_kernel,
        grid=(b,),
        in_specs=[pl.BlockSpec((1, m, d), lambda i: (i, 0, 0)), _const_spec((1, d)), _const_spec((d, n))],
        out_specs=(pl.BlockSpec((1, m, CROSS_W), lambda i: (i, 0, 0)),
                   pl.BlockSpec((1, N_CROSS_HEADS, PV_ROWS, m), lambda i: (i, 0, 0, 0))),
        out_shape=(jax.ShapeDtypeStruct((b, m, CROSS_W), BF16),
                   jax.ShapeDtypeStruct((b, N_CROSS_HEADS, PV_ROWS, m), BF16)),
        compiler_params=_cparams("parallel"),
        name="mem_kv",
    )(mem, gain, w)


def _merge_kernel(x_ref, a_ref, y_ref, u_ref, qct_ref, gate_ref, mk_ref, mvt_ref, dskip_ref, wglu_ref,
                  pa_ref, ps_ref, pc_ref, wout_ref, o_ref):
    d = x_ref.shape[2]
    heads = []
    for h in range(N_CROSS_HEADS):
        mk = _head_lanes(mk_ref[0, :, (h // 2) * LANES:(h // 2 + 1) * LANES], h)
        st = jnp.dot(mk, qct_ref[0, h * HEAD_DIM:(h + 1) * HEAD_DIM, :], preferred_element_type=F32)
        pt = jnp.exp(st - jnp.max(st, axis=0, keepdims=True)).astype(BF16)
        ot = jnp.dot(mvt_ref[0, h], pt, preferred_element_type=F32)
        heads.append(ot[:HEAD_DIM] / ot[HEAD_DIM:HEAD_DIM + 1])
    cross = jnp.concatenate(heads, axis=0).T.astype(BF16)
    y = y_ref[0].T + dskip_ref[...] * u_ref[0].astype(F32)
    y = jax.nn.gelu(y)
    y = y * jax.nn.sigmoid(jnp.dot(y.astype(BF16), wglu_ref[...], preferred_element_type=F32))
    gate = lambda j: gate_ref[0, :, j * d:(j + 1) * d].astype(F32)
    m = gate(0) * jnp.dot(a_ref[0], pa_ref[...], preferred_element_type=F32)
    m = m + gate(1) * jnp.dot(y.astype(BF16), ps_ref[...], preferred_element_type=F32)
    m = m + gate(2) * jnp.dot(cross, pc_ref[...], preferred_element_type=F32)
    o_ref[0] = x_ref[0] + jnp.dot(m.astype(BF16), wout_ref[...], preferred_element_type=F32)


def _merge(x, attn, yscan, u, qct, gates, mk, mvt, d_skip, w_glu, p_attn, p_ssm, p_cross, w_out, tl):
    b, l, d = x.shape
    row = lambda width: pl.BlockSpec((1, tl, width), lambda i, j: (i, j, 0))
    col = lambda height: pl.BlockSpec((1, height, tl), lambda i, j: (i, 0, j))
    n_mem = mk.shape[1]
    return pl.pallas_call(
        _merge_kernel,
        grid=(b, l // tl),
        in_specs=[row(d), row(ATTN_W), col(SSM_W), row(SSM_W), col(CROSS_W), row(N_BRANCHES * d),
                  pl.BlockSpec((1, n_mem, CROSS_W), lambda i, j: (i, 0, 0)),
                  pl.BlockSpec((1, N_CROSS_HEADS, PV_ROWS, n_mem), lambda i, j: (i, 0, 0, 0)),
                  _const_spec((1, SSM_W)), _const_spec((SSM_W, SSM_W)),
                  _const_spec((ATTN_W, d)), _const_spec((SSM_W, d)), _const_spec((CROSS_W, d)),
                  _const_spec((d, d))],
        out_specs=row(d),
        out_shape=jax.ShapeDtypeStruct((b, l, d), F32),
        compiler_params=_cparams("parallel", "parallel"),
        name="merge",
    )(x, attn, yscan, u, qct, gates, mk, mvt, d_skip, w_glu, p_attn, p_ssm, p_cross, w_out)


MXU_TILE = 256


def _ffn_chunks(dff, tiles_per_chunk=11):
    step = tiles_per_chunk * MXU_TILE
    return [min(step, dff - c0) for c0 in range(0, dff, step)]


def _ffn_kernel(xp_ref, x_ref, xn_ref, g_ref, wup_ref, cw_ref, cb_ref, wdn_ref, gf_ref, o_ref,
                h_scr, gate_scr, *, final_norm):
    j = pl.program_id(1)
    nj = pl.num_programs(1)
    tm = x_ref.shape[1]
    dff = wdn_ref.shape[0]
    gain = g_ref[...]
    x = x_ref[0]
    h_scr[0:HALO, :] = jnp.where(j == 0, 0.0, _rms(xp_ref[0], gain)).astype(BF16)
    h_scr[HALO:HALO + tm, :] = _rms(x, gain).astype(BF16)
    h_scr[HALO + tm:, :] = jnp.where(j == nj - 1, 0.0, _rms(xn_ref[0], gain)).astype(BF16)
    y = x
    c0 = 0
    for wc in _ffn_chunks(dff):
        gs = gate_scr.at[:, c0:c0 + wc]
        gs[...] = jnp.dot(h_scr[...], wup_ref[:, c0:c0 + wc], preferred_element_type=F32)
        val = jnp.dot(h_scr[HALO:HALO + tm, :], wup_ref[:, dff + c0:dff + c0 + wc], preferred_element_type=F32)
        cw = cw_ref[:, c0:c0 + wc]
        gc = (gs[HALO - 1:HALO - 1 + tm, :] * cw[0:1, :] + gs[HALO:HALO + tm, :] * cw[1:2, :]
              + gs[HALO + 1:HALO + 1 + tm, :] * cw[2:3, :] + cb_ref[:, c0:c0 + wc])
        act = (jax.nn.gelu(gc) * val).astype(BF16)
        y = y + jnp.dot(act, wdn_ref[c0:c0 + wc, :], preferred_element_type=F32)
        c0 += wc
    if final_norm:
        y = _rms(y, gf_ref[...])
    o_ref[0] = y


def _ffn(x, gain, w_up, conv_w, conv_b, w_down, final_gain, final_norm, tm):
    b, l, d = x.shape
    dff = w_down.shape[0]
    r = tm // HALO
    nh = l // HALO
    return pl.pallas_call(
        functools.partial(_ffn_kernel, final_norm=final_norm),
        grid=(b, l // tm),
        in_specs=[pl.BlockSpec((1, HALO, d), lambda i, j: (i, jnp.maximum(j * r - 1, 0), 0)),
                  pl.BlockSpec((1, tm, d), lambda i, j: (i, j, 0)),
                  pl.BlockSpec((1, HALO, d), lambda i, j: (i, jnp.minimum((j + 1) * r, nh - 1), 0)),
                  _const_spec((1, d)), _const_spec((d, 2 * dff)), _const_spec((3, dff)),
                  _const_spec((1, dff)), _const_spec((dff, d)), _const_spec((1, d))],
        out_specs=pl.BlockSpec((1, tm, d), lambda i, j: (i, j, 0)),
        out_shape=jax.ShapeDtypeStruct((b, l, d), F32),
        scratch_shapes=[pltpu.VMEM((tm + 2 * HALO, d), BF16),
                        pltpu.VMEM((tm + 2 * HALO, dff), F32)],
        compiler_params=_cparams("parallel", "arbitrary"),
        name="ffn",
    )(x, x, x, gain, w_up, conv_w, conv_b, w_down, final_gain)


def _rope_tables(l):
    rows = l // GRID_W
    r = jnp.broadcast_to(jnp.arange(rows, dtype=F32)[:, None], (rows, GRID_W)).reshape(l)
    c = jnp.broadcast_to(jnp.arange(GRID_W, dtype=F32)[None, :], (rows, GRID_W)).reshape(l)
    freqs = ROPE_THETA ** (-jnp.arange(ROPE_PAIRS, dtype=F32) / ROPE_PAIRS)
    ang_r, ang_c = r[:, None] * freqs, c[:, None] * freqs
    ang = jnp.concatenate([ang_r, ang_r, ang_c, ang_c], axis=-1)
    cos, sin = jnp.cos(ang), jnp.sin(ang)
    first = (jnp.arange(HEAD_DIM) % (2 * ROPE_PAIRS)) < ROPE_PAIRS
    sin_up = jnp.where(first, -sin, 0.0)
    sin_dn = jnp.where(first, 0.0, sin)
    two = lambda a: jnp.concatenate([a, a], axis=1)
    return two(cos), two(sin_up), two(sin_dn)


def _prepare_layer(p, l):
    bf = lambda a: a.astype(BF16)
    row = lambda a: a.astype(F32).reshape(1, -1)
    max_abs = lambda a: jnp.max(jnp.abs(a.astype(F32)))
    score_bound = 1.02 * QK_SCALE * HEAD_DIM * max_abs(p['q_norm'][l]) * max_abs(p['k_norm'][l])
    return dict(
        norm_mix=row(p['norm_mix'][l]), w_in=bf(p['w_in'][l]),
        q_gain=row(jnp.tile(p['q_norm'][l], N_HEADS)), k_gain=row(jnp.tile(p['k_norm'][l], N_KV_HEADS)),
        score_bound=score_bound,
        s5=_s5_operators(p['ssm_a_re'][l], p['ssm_a_im'][l], p['ssm_log_dt'][l], p['ssm_b_re'][l],
                         p['ssm_b_im'][l], p['ssm_c_re'][l], p['ssm_c_im'][l], S5_CHUNK),
        d_skip=row(p['ssm_d'][l]), w_glu=bf(p['ssm_glu'][l]),
        mem_norm=row(p['mem_norm'][l]), w_mem_kv=bf(p['w_mem_kv'][l]),
        p_attn=bf(p['p_attn'][l]), p_ssm=bf(p['p_ssm'][l]), p_cross=bf(p['p_cross'][l]), w_out=bf(p['w_out'][l]),
        norm_ffn=row(p['norm_ffn'][l]), w_up=bf(p['w_up'][l]), conv_w=p['conv_w'][l].astype(F32),
        conv_b=row(p['conv_b'][l]), w_down=bf(p['w_down'][l]),
    )


def _encode(x, mem, layers, final_gain):
    b, l, d = x.shape
    tables = _rope_tables(l)
    tk = _pick(l, 512)
    tq = _pick(l, 1024)
    tm = _pick(l, 512)
    for li, w in enumerate(layers):
        qt, k, vt, u, ut, qct, gates = _in_proj(x, w['norm_mix'], w['w_in'], tables, w['q_gain'], w['k_gain'], tk)
        attn = _attention(qt, k, vt, tq, w['score_bound'])
        yscan = _s5_scan_branch(ut, w['s5'])
        mk, mvt = _mem_kv(mem, w['mem_norm'], w['w_mem_kv'])
        x = _merge(x, attn, yscan, u, qct, gates, mk, mvt, w['d_skip'], w['w_glu'], w['p_attn'], w['p_ssm'],
                   w['p_cross'], w['w_out'], tm)
        x = _ffn(x, w['norm_ffn'], w['w_up'], w['conv_w'], w['conv_b'], w['w_down'], final_gain,
                 li == len(layers) - 1, tm)
    return x


def kernel(x_prompt, x_sample, mem_prompt, mem_sample, norm_mix, w_in, q_norm, k_norm, ssm_a_re, ssm_a_im, ssm_log_dt, ssm_b_re, ssm_b_im, ssm_c_re, ssm_c_im, ssm_d, ssm_glu, mem_norm, w_mem_kv, p_attn, p_ssm, p_cross, w_out, norm_ffn, w_up, conv_w, conv_b, w_down, norm_final):
    p = dict(norm_mix=norm_mix, w_in=w_in, q_norm=q_norm, k_norm=k_norm, ssm_a_re=ssm_a_re, ssm_a_im=ssm_a_im,
             ssm_log_dt=ssm_log_dt, ssm_b_re=ssm_b_re, ssm_b_im=ssm_b_im, ssm_c_re=ssm_c_re, ssm_c_im=ssm_c_im,
             ssm_d=ssm_d, ssm_glu=ssm_glu, mem_norm=mem_norm, w_mem_kv=w_mem_kv, p_attn=p_attn, p_ssm=p_ssm,
             p_cross=p_cross, w_out=w_out, norm_ffn=norm_ffn, w_up=w_up, conv_w=conv_w, conv_b=conv_b,
             w_down=w_down)
    layers = [_prepare_layer(p, l) for l in range(norm_mix.shape[0])]
    final_gain = norm_final.astype(F32).reshape(1, -1)
    y_prompt = _encode(x_prompt, mem_prompt, layers, final_gain)
    y_sample = _encode(x_sample, mem_sample, layers, final_gain)
    return (y_prompt, y_sample)
```

```python
import functools
import math

import jax
import jax.numpy as jnp
from jax import lax
from jax.experimental import pallas as pl
from jax.experimental.pallas import tpu as pltpu

HEAD_DIM = 64
N_HEADS = 8
N_KV_HEADS = 2
KV_GROUP = N_HEADS // N_KV_HEADS
ATTN_W = N_HEADS * HEAD_DIM
KV_W = N_KV_HEADS * HEAD_DIM
SSM_GROUP_CH = 16
SSM_W = 256
SSM_GROUPS = SSM_W // SSM_GROUP_CH
SSM_STATE = 64
N_CROSS_HEADS = 4
CROSS_W = N_CROSS_HEADS * HEAD_DIM
N_BRANCHES = 3
GRID_W = 64
ROPE_THETA = 10000.0
ROPE_PAIRS = HEAD_DIM // 4
EPS = 1e-6

LOG2E = 1.4426950408889634
QK_SCALE = HEAD_DIM ** -0.5 * LOG2E
SAFE_SCORE = 64.0
LANES = 128
S5_CHUNK = LANES
PV_ROWS = 2 * HEAD_DIM
HALO = 16
VMEM_LIMIT = 56 * 1024 * 1024

F32 = jnp.float32
BF16 = jnp.bfloat16


def _cparams(*sem):
    return pltpu.CompilerParams(dimension_semantics=sem, vmem_limit_bytes=VMEM_LIMIT)


def _const_spec(shape):
    nd = len(shape)
    return pl.BlockSpec(shape, lambda *_: (0,) * nd, pipeline_mode=pl.Buffered(1))


def _rms(x, gain):
    return x * lax.rsqrt(jnp.mean(x * x, axis=-1, keepdims=True) + EPS) * gain


def _sigmoid(x):
    return 0.5 * jnp.tanh(0.5 * x) + 0.5


def _pick(n, pref):
    t = min(n, pref)
    while n % t:
        t //= 2
    return t


def _head_norm_rope(x, gain, cos, sin_up, sin_dn, out_scale):
    w = x.shape[1]
    reps = w // LANES
    tile = lambda a: a if reps == 1 else jnp.concatenate([a] * reps, axis=1)
    xg = x * gain
    y = (xg * tile(cos)
         + pltpu.roll(xg, w - ROPE_PAIRS, 1) * tile(sin_up)
         + pltpu.roll(xg, ROPE_PAIRS, 1) * tile(sin_dn))
    lane = lax.broadcasted_iota(jnp.int32, (1, LANES), 1)
    low = lane < HEAD_DIM
    outs = []
    for p in range(reps):
        xp = x[:, p * LANES:(p + 1) * LANES]
        sq = xp * xp
        ss_all = jnp.sum(sq, axis=-1, keepdims=True)
        ss_lo = jnp.sum(jnp.where(low, sq, 0.0), axis=-1, keepdims=True)
        r_lo = lax.rsqrt(ss_lo * (1.0 / HEAD_DIM) + EPS) * out_scale
        r_hi = lax.rsqrt((ss_all - ss_lo) * (1.0 / HEAD_DIM) + EPS) * out_scale
        outs.append(y[:, p * LANES:(p + 1) * LANES] * jnp.where(low, r_lo, r_hi))
    return outs[0] if reps == 1 else jnp.concatenate(outs, axis=1)


def _in_proj_kernel(x_ref, g_ref, w_ref, cos_ref, sup_ref, sdn_ref, qg_ref, kg_ref,
                    qt_ref, k_ref, vt_ref, u_ref, ut_ref, qc_ref, gate_ref, h_scr):
    h_scr[...] = _rms(x_ref[0], g_ref[...]).astype(BF16)
    z = jnp.dot(h_scr[...], w_ref[...], preferred_element_type=F32)

    def seg(a, b):
        return z[:, a:b]

    cos, sup, sdn = cos_ref[...], sup_ref[...], sdn_ref[...]
    o = 0
    q = seg(o, o + ATTN_W)
    qt_ref[0] = _head_norm_rope(q, qg_ref[...], cos, sup, sdn, QK_SCALE).T.astype(BF16)
    o += ATTN_W
    kv = seg(o, o + 2 * KV_W)
    k_ref[0] = _head_norm_rope(kv[:, :KV_W], kg_ref[...], cos, sup, sdn, 1.0).astype(BF16)
    vt = kv[:, KV_W:].T.astype(BF16)
    ones = jnp.ones((PV_ROWS - HEAD_DIM, vt.shape[1]), BF16)
    for g in range(N_KV_HEADS):
        vt_ref[0, 0, g, :HEAD_DIM, :] = vt[g * HEAD_DIM:(g + 1) * HEAD_DIM, :]
        vt_ref[0, 0, g, HEAD_DIM:, :] = ones
    o += 2 * KV_W
    u = seg(o, o + SSM_W)
    u_ref[0] = u.astype(BF16)
    ut_ref[0] = u.T.astype(BF16)
    o += SSM_W
    qc_ref[0] = (seg(o, o + CROSS_W) * HEAD_DIM ** -0.5).T.astype(BF16)
    o += CROSS_W
    d = x_ref.shape[2]
    for j in range(N_BRANCHES):
        gate_ref[0, :, j * d:(j + 1) * d] = _sigmoid(seg(o + j * d, o + (j + 1) * d)).astype(BF16)


def _in_proj(x, gain, w, tables, q_gain, k_gain, tk):
    b, l, d = x.shape
    n_in = w.shape[1]
    nj = l // tk
    row = lambda width: pl.BlockSpec((1, tk, width), lambda i, j: (i, j, 0))
    tab = pl.BlockSpec((tk, LANES), lambda i, j: (j, 0))
    out_shape = (
        jax.ShapeDtypeStruct((b, ATTN_W, l), BF16),
        jax.ShapeDtypeStruct((b, l, KV_W), BF16),
        jax.ShapeDtypeStruct((b, nj, N_KV_HEADS, PV_ROWS, tk), BF16),
        jax.ShapeDtypeStruct((b, l, SSM_W), BF16),
        jax.ShapeDtypeStruct((b, SSM_W, l), BF16),
        jax.ShapeDtypeStruct((b, CROSS_W, l), BF16),
        jax.ShapeDtypeStruct((b, l, N_BRANCHES * d), BF16),
    )
    col = lambda height: pl.BlockSpec((1, height, tk), lambda i, j: (i, 0, j))
    return pl.pallas_call(
        _in_proj_kernel,
        grid=(b, nj),
        in_specs=[row(d), _const_spec((1, d)), _const_spec((d, n_in)), tab, tab, tab,
                  _const_spec((1, ATTN_W)), _const_spec((1, KV_W))],
        out_specs=(col(ATTN_W), row(KV_W),
                   pl.BlockSpec((1, 1, N_KV_HEADS, PV_ROWS, tk), lambda i, j: (i, j, 0, 0, 0)),
                   row(SSM_W), col(SSM_W), col(CROSS_W), row(N_BRANCHES * d)),
        out_shape=out_shape,
        scratch_shapes=[pltpu.VMEM((tk, d), BF16)],
        compiler_params=_cparams("parallel", "parallel"),
        name="in_proj",
    )(x, gain, w, *tables, q_gain, k_gain)


def _head_lanes(pair, h):
    return pair[:, (h % 2) * HEAD_DIM:(h % 2 + 1) * HEAD_DIM]


def _attn_kernel(qt_ref, k_ref, vt_ref, o_ref, qs_ref, acc_ref, m_ref, *, bounded):
    tq = qt_ref.shape[2]
    nc, tk = vt_ref.shape[1], vt_ref.shape[4]
    qs_ref[...] = jnp.zeros(qs_ref.shape, BF16)
    for h in range(N_HEADS):
        g, r = divmod(h, KV_GROUP)
        qs_ref[g, g * HEAD_DIM:(g + 1) * HEAD_DIM, r * tq:(r + 1) * tq] = qt_ref[0, h * HEAD_DIM:(h + 1) * HEAD_DIM, :]
    acc_ref[...] = jnp.zeros(acc_ref.shape, F32)
    if not bounded:
        m_ref[...] = jnp.full(m_ref.shape, -jnp.inf, F32)

    def body(c, carry):
        kc = k_ref[0, pl.ds(pl.multiple_of(c * tk, tk), tk), :]
        for g in range(N_KV_HEADS):
            st = jnp.dot(kc, qs_ref[g], preferred_element_type=F32)
            if bounded:
                pt = jnp.exp2(st).astype(BF16)
                acc_ref[g] += jnp.dot(vt_ref[0, c, g], pt, preferred_element_type=F32)
            else:
                m_old = m_ref[g]
                m_new = jnp.maximum(m_old, jnp.max(st, axis=0, keepdims=True))
                pt = jnp.exp2(st - m_new).astype(BF16)
                acc_ref[g] = (jnp.exp2(m_old - m_new) * acc_ref[g]
                              + jnp.dot(vt_ref[0, c, g], pt, preferred_element_type=F32))
                m_ref[g] = m_new
        return carry

    lax.fori_loop(0, nc, body, 0, unroll=4)
    for h in range(0, N_HEADS, 2):
        g, r = divmod(h, KV_GROUP)
        acc = acc_ref[g, :, r * tq:(r + 2) * tq]
        ot = acc[:HEAD_DIM] / acc[HEAD_DIM:HEAD_DIM + 1]
        pair = jnp.concatenate([ot[:, :tq], ot[:, tq:]], axis=0)
        o_ref[0, :, h * HEAD_DIM:(h + 2) * HEAD_DIM] = pair.T.astype(BF16)


def _attention(qt, k, vt, tq, score_bound):
    b, _, l = qt.shape
    nc, tk = vt.shape[1], vt.shape[4]

    def call(bounded):
        return pl.pallas_call(
            functools.partial(_attn_kernel, bounded=bounded),
            grid=(b, l // tq),
            in_specs=[pl.BlockSpec((1, ATTN_W, tq), lambda i, j: (i, 0, j)),
                      pl.BlockSpec((1, l, KV_W), lambda i, j: (i, 0, 0)),
                      pl.BlockSpec((1, nc, N_KV_HEADS, PV_ROWS, tk), lambda i, j: (i, 0, 0, 0, 0))],
            out_specs=pl.BlockSpec((1, tq, ATTN_W), lambda i, j: (i, j, 0)),
            out_shape=jax.ShapeDtypeStruct((b, l, ATTN_W), BF16),
            scratch_shapes=[pltpu.VMEM((N_KV_HEADS, KV_W, KV_GROUP * tq), BF16),
                            pltpu.VMEM((N_KV_HEADS, PV_ROWS, KV_GROUP * tq), F32),
                            pltpu.VMEM((N_KV_HEADS, 1, KV_GROUP * tq), F32)],
            compiler_params=_cparams("parallel", "parallel"),
            name="attention" if bounded else "attention_running_max",
        )

    return lax.cond(score_bound <= SAFE_SCORE, call(True), call(False), qt, k, vt)


def _complex_powers(lr, li, n):
    pr, pi = jnp.ones_like(lr)[None], jnp.zeros_like(li)[None]
    cr, ci = lr, li
    while pr.shape[0] < n:
        pr, pi = (jnp.concatenate([pr, pr * cr - pi * ci]), jnp.concatenate([pi, pr * ci + pi * cr]))
        cr, ci = cr * cr - ci * ci, 2.0 * cr * ci
    return pr[:n], pi[:n]


def _toeplitz_kernel(k_ref, o_ref):
    n, hch, t = k_ref.shape[1], k_ref.shape[2], k_ref.shape[3] // 2
    for i in range(n):
        for h in range(hch):
            rows = jnp.broadcast_to(k_ref[0, i, h:h + 1, :], (t, 2 * t))
            rolled = pltpu.roll(rows, t + 1, 1, stride=1, stride_axis=0)
            o_ref[0, i * t:(i + 1) * t, h * t:(h + 1) * t] = rolled[:, :t].astype(BF16)


def _toeplitz(k_cat):
    g, hin, hout, t2 = k_cat.shape
    t = t2 // 2
    n = _pick(hin, 4)
    return pl.pallas_call(
        _toeplitz_kernel,
        grid=(g, hin // n),
        in_specs=[pl.BlockSpec((1, n, hout, t2), lambda i, j: (i, j, 0, 0))],
        out_specs=pl.BlockSpec((1, n * t, hout * t), lambda i, j: (i, j, 0)),
        out_shape=jax.ShapeDtypeStruct((g, hin * t, hout * t), BF16),
        compiler_params=_cparams("parallel", "parallel"),
        name="s5_toeplitz",
    )(k_cat)


def _s5_operators(a_re, a_im, log_dt, b_re, b_im, c_re, c_im, t):
    hi = lax.Precision.HIGHEST
    a_re, a_im, log_dt = a_re.astype(F32), a_im.astype(F32), log_dt.astype(F32)
    b_re, b_im, c_re, c_im = (z.astype(F32) for z in (b_re, b_im, c_re, c_im))
    dt = jnp.exp(log_dt)[..., None]
    mag = jnp.exp(a_re * dt)
    lam_re, lam_im = mag * jnp.cos(a_im * dt), mag * jnp.sin(a_im * dt)
    num_re = lam_re - 1.0
    den = a_re * a_re + a_im * a_im
    coef_re = (num_re * a_re + lam_im * a_im) / den
    coef_im = (lam_im * a_re - num_re * a_im) / den
    pw_re, pw_im = _complex_powers(lam_re, lam_im, t + 1)
    e_re = pw_re * coef_re - pw_im * coef_im
    e_im = pw_re * coef_im + pw_im * coef_re
    g, p, hch = b_re.shape

    cb_re = c_re[:, :, :, None] * b_re[:, None, :, :] - c_im[:, :, :, None] * b_im[:, None, :, :]
    cb_im = c_re[:, :, :, None] * b_im[:, None, :, :] + c_im[:, :, :, None] * b_re[:, None, :, :]
    kern = (jnp.einsum('tdgp,ghpk->dgthk', e_re[:t], cb_re, precision=hi)
            - jnp.einsum('tdgp,ghpk->dgthk', e_im[:t], cb_im, precision=hi))
    k_f, k_b = kern[0], kern[1]
    k_cat = jnp.concatenate([k_b[:, :0:-1], (k_f[:, :1] + k_b[:, :1]), k_f[:, 1:]], axis=1)
    k_cat = jnp.pad(k_cat.transpose(0, 3, 2, 1), ((0, 0), (0, 0), (0, 0), (0, 1)))
    toep = _toeplitz(k_cat)

    bt_re, bt_im = b_re.transpose(0, 2, 1)[:, :, None, :], b_im.transpose(0, 2, 1)[:, :, None, :]

    def times_b(er, ei):
        er, ei = er.transpose(1, 0, 2)[:, None], ei.transpose(1, 0, 2)[:, None]
        re = er * bt_re - ei * bt_im
        im = er * bt_im + ei * bt_re
        f = lambda z: z.reshape(g, hch * t, p)
        return f(re), f(im)
    sf_re, sf_im = times_b(e_re[:t, 0][::-1], e_im[:t, 0][::-1])
    sb_re, sb_im = times_b(e_re[:t, 1], e_im[:t, 1])
    w_s = jnp.concatenate([sf_re, sf_im, sb_re, sb_im], axis=2)

    ct_re, ct_im = c_re.transpose(0, 2, 1)[:, :, :, None], c_im.transpose(0, 2, 1)[:, :, :, None]

    def c_times(pr, pi):
        pr, pi = pr.transpose(1, 2, 0)[:, :, None, :], pi.transpose(1, 2, 0)[:, :, None, :]
        re = ct_re * pr - ct_im * pi
        im = ct_re * pi + ct_im * pr
        f = lambda z: z.reshape(g, p, hch * t)
        return f(re), f(-im)
    of_re, of_im = c_times(pw_re[1:t + 1, 0], pw_im[1:t + 1, 0])
    ob_re, ob_im = c_times(pw_re[1:t + 1, 1][::-1], pw_im[1:t + 1, 1][::-1])
    w_o = jnp.concatenate([of_re, of_im, ob_re, ob_im], axis=1)

    lt_re, lt_im = pw_re[t], pw_im[t]
    a1 = jnp.concatenate([lt_re[0], lt_re[0], lt_re[1], lt_re[1]], axis=-1)[:, None, :]
    a2 = jnp.concatenate([-lt_im[0], lt_im[0], -lt_im[1], lt_im[1]], axis=-1)[:, None, :]
    return toep.astype(BF16), w_s.astype(BF16), w_o.astype(BF16), a1, a2


def _s5_chunk_rows(u_ref):
    b, hch, c, t = u_ref.shape
    return jnp.concatenate([u_ref[:, h].reshape(b * c, t) for h in range(hch)], axis=1)


def _s5_state_kernel(u_ref, ws_ref, s_ref):
    s_ref[0] = jnp.dot(_s5_chunk_rows(u_ref), ws_ref[0], preferred_element_type=F32)


def _s5_state(ut, w_s):
    b, ch, c, t = ut.shape
    g, k, n = w_s.shape
    return pl.pallas_call(
        _s5_state_kernel,
        grid=(g,),
        in_specs=[pl.BlockSpec((b, ch // g, c, t), lambda i: (0, i, 0, 0)),
                  pl.BlockSpec((1, k, n), lambda i: (i, 0, 0))],
        out_specs=pl.BlockSpec((1, b * c, n), lambda i: (i, 0, 0)),
        out_shape=jax.ShapeDtypeStruct((g, b * c, n), F32),
        compiler_params=_cparams("parallel"),
        name="s5_state",
    )(ut, w_s)


def _s5_scan_kernel(s_ref, a1_ref, a2_ref, h_ref, *, nb):
    gb, rows, _ = s_ref.shape
    nchunk = rows // (2 * nb)
    half = 2 * SSM_STATE
    a1, a2 = a1_ref[...], a2_ref[...]
    a1f, a1b, a2f, a2b = a1[:, :, :half], a1[:, :, half:], a2[:, :, :half], a2[:, :, half:]

    def swap(z):
        return pltpu.roll(z.reshape(gb * nb, half), SSM_STATE, 1).reshape(gb, nb, half)

    def body(j, carry):
        hf, hb = carry
        rf = pl.ds(2 * j, nb, stride=2 * nchunk)
        rb = pl.ds(2 * (nchunk - 1 - j) + 1, nb, stride=2 * nchunk)
        h_ref[:, rf, :] = hf
        h_ref[:, rb, :] = hb
        hf = a1f * hf + a2f * swap(hf) + s_ref[:, rf, :]
        hb = a1b * hb + a2b * swap(hb) + s_ref[:, rb, :]
        return hf, hb

    zero = jnp.zeros((gb, nb, half), F32)
    lax.fori_loop(0, nchunk, body, (zero, zero))


def _s5_scan(s, a1, a2, nb, gb):
    g, r, n = s.shape
    half = n // 2
    blk = pl.BlockSpec((gb, 2 * r, half), lambda i: (i, 0, 0))
    par = pl.BlockSpec((gb, 1, n), lambda i: (i, 0, 0))
    return pl.pallas_call(
        functools.partial(_s5_scan_kernel, nb=nb),
        grid=(g // gb,),
        in_specs=[blk, par, par],
        out_specs=blk,
        out_shape=jax.ShapeDtypeStruct((g, 2 * r, half), F32),
        compiler_params=_cparams("parallel"),
        name="s5_scan",
    )(s.reshape(g, 2 * r, half), a1, a2).reshape(g, r, n)


def _s5_out_kernel(u_ref, h_ref, toep_ref, wo_ref, y_ref):
    b, hch, c, t = u_ref.shape
    y = jnp.dot(_s5_chunk_rows(u_ref), toep_ref[0], preferred_element_type=F32)
    y = y + jnp.dot(h_ref[0].astype(BF16), wo_ref[0], preferred_element_type=F32)
    for h in range(hch):
        y_ref[:, h] = y[:, h * t:(h + 1) * t].reshape(b, c, t)


def _s5_out(ut, hin, toep, w_o):
    b, ch, c, t = ut.shape
    g, n, k = w_o.shape
    blk = pl.BlockSpec((b, ch // g, c, t), lambda i: (0, i, 0, 0))
    return pl.pallas_call(
        _s5_out_kernel,
        grid=(g,),
        in_specs=[blk, pl.BlockSpec((1, b * c, n), lambda i: (i, 0, 0)),
                  pl.BlockSpec((1, k, k), lambda i: (i, 0, 0)),
                  pl.BlockSpec((1, n, k), lambda i: (i, 0, 0))],
        out_specs=blk,
        out_shape=jax.ShapeDtypeStruct((b, ch, c, t), F32),
        compiler_params=_cparams("parallel"),
        name="s5_out",
    )(ut, hin, toep, w_o)


def _s5_scan_branch(ut, ops):
    toep, w_s, w_o, a1, a2 = ops
    b, ch, l = ut.shape
    t = toep.shape[1] // SSM_GROUP_CH
    c = l // t
    g, n = w_s.shape[0], w_s.shape[2]
    ut = ut.reshape(b, ch, c, t)
    s = _s5_state(ut, w_s)
    hin = _s5_scan(s, a1, a2, b, 4)
    return _s5_out(ut, hin, toep, w_o).reshape(b, ch, l)


def _mem_kv_kernel(m_ref, g_ref, w_ref, mk_ref, mvt_ref):
    h = _rms(m_ref[0], g_ref[...]).astype(BF16)
    kv = jnp.dot(h, w_ref[...], preferred_element_type=F32)
    mk_ref[0] = kv[:, :CROSS_W].astype(BF16)
    vt = kv[:, CROSS_W:].T.astype(BF16)
    ones = jnp.ones((PV_ROWS - HEAD_DIM, vt.shape[1]), BF16)
    for hd in range(N_CROSS_HEADS):
        mvt_ref[0, hd, :HEAD_DIM, :] = vt[hd * HEAD_DIM:(hd + 1) * HEAD_DIM, :]
        mvt_ref[0, hd, HEAD_DIM:, :] = ones


def _mem_kv(mem, gain, w):
    b, m, d = mem.shape
    n = w.shape[1]
    return pl.pallas_call(
        _mem_kv_kernel,
        grid=(b,),
        in_specs=[pl.BlockSpec((1, m, d), lambda i: (i, 0, 0)), _const_spec((1, d)), _const_spec((d, n))],
        out_specs=(pl.BlockSpec((1, m, CROSS_W), lambda i: (i, 0, 0)),
                   pl.BlockSpec((1, N_CROSS_HEADS, PV_ROWS, m), lambda i: (i, 0, 0, 0))),
        out_shape=(jax.ShapeDtypeStruct((b, m, CROSS_W), BF16),
                   jax.ShapeDtypeStruct((b, N_CROSS_HEADS, PV_ROWS, m), BF16)),
        compiler_params=_cparams("parallel"),
        name="mem_kv",
    )(mem, gain, w)


def _merge_kernel(x_ref, a_ref, y_ref, u_ref, qct_ref, gate_ref, mk_ref, mvt_ref, dskip_ref, wglu_ref,
                  pa_ref, ps_ref, pc_ref, wout_ref, o_ref):
    d = x_ref.shape[2]
    heads = []
    for h in range(N_CROSS_HEADS):
        mk = _head_lanes(mk_ref[0, :, (h // 2) * LANES:(h // 2 + 1) * LANES], h)
        st = jnp.dot(mk, qct_ref[0, h * HEAD_DIM:(h + 1) * HEAD_DIM, :], preferred_element_type=F32)
        pt = jnp.exp(st - jnp.max(st, axis=0, keepdims=True)).astype(BF16)
        ot = jnp.dot(mvt_ref[0, h], pt, preferred_element_type=F32)
        heads.append(ot[:HEAD_DIM] / ot[HEAD_DIM:HEAD_DIM + 1])
    cross = jnp.concatenate(heads, axis=0).T.astype(BF16)
    y = y_ref[0].T + dskip_ref[...] * u_ref[0].astype(F32)
    y = jax.nn.gelu(y)
    y = y * _sigmoid(jnp.dot(y.astype(BF16), wglu_ref[...], preferred_element_type=F32))
    gate = lambda j: gate_ref[0, :, j * d:(j + 1) * d].astype(F32)
    m = gate(0) * jnp.dot(a_ref[0], pa_ref[...], preferred_element_type=F32)
    m = m + gate(1) * jnp.dot(y.astype(BF16), ps_ref[...], preferred_element_type=F32)
    m = m + gate(2) * jnp.dot(cross, pc_ref[...], preferred_element_type=F32)
    o_ref[0] = x_ref[0] + jnp.dot(m.astype(BF16), wout_ref[...], preferred_element_type=F32)


def _merge(x, attn, yscan, u, qct, gates, mk, mvt, d_skip, w_glu, p_attn, p_ssm, p_cross, w_out, tl):
    b, l, d = x.shape
    row = lambda width: pl.BlockSpec((1, tl, width), lambda i, j: (i, j, 0))
    col = lambda height: pl.BlockSpec((1, height, tl), lambda i, j: (i, 0, j))
    n_mem = mk.shape[1]
    return pl.pallas_call(
        _merge_kernel,
        grid=(b, l // tl),
        in_specs=[row(d), row(ATTN_W), col(SSM_W), row(SSM_W), col(CROSS_W), row(N_BRANCHES * d),
                  pl.BlockSpec((1, n_mem, CROSS_W), lambda i, j: (i, 0, 0)),
                  pl.BlockSpec((1, N_CROSS_HEADS, PV_ROWS, n_mem), lambda i, j: (i, 0, 0, 0)),
                  _const_spec((1, SSM_W)), _const_spec((SSM_W, SSM_W)),
                  _const_spec((ATTN_W, d)), _const_spec((SSM_W, d)), _const_spec((CROSS_W, d)),
                  _const_spec((d, d))],
        out_specs=row(d),
        out_shape=jax.ShapeDtypeStruct((b, l, d), F32),
        compiler_params=_cparams("parallel", "parallel"),
        name="merge",
    )(x, attn, yscan, u, qct, gates, mk, mvt, d_skip, w_glu, p_attn, p_ssm, p_cross, w_out)


MXU_TILE = 256


def _ffn_chunks(dff, tiles_per_chunk=11):
    step = tiles_per_chunk * MXU_TILE
    return [min(step, dff - c0) for c0 in range(0, dff, step)]


def _ffn_kernel(xp_ref, x_ref, xn_ref, g_ref, wup_ref, cw_ref, cb_ref, wdn_ref, gf_ref, o_ref,
                h_scr, gate_scr, *, final_norm):
    j = pl.program_id(1)
    nj = pl.num_programs(1)
    tm = x_ref.shape[1]
    dff = wdn_ref.shape[0]
    gain = g_ref[...]
    x = x_ref[0]
    h_scr[0:HALO, :] = jnp.where(j == 0, 0.0, _rms(xp_ref[0], gain)).astype(BF16)
    h_scr[HALO:HALO + tm, :] = _rms(x, gain).astype(BF16)
    h_scr[HALO + tm:, :] = jnp.where(j == nj - 1, 0.0, _rms(xn_ref[0], gain)).astype(BF16)
    y = x
    c0 = 0
    for wc in _ffn_chunks(dff):
        gs = gate_scr.at[:, c0:c0 + wc]
        gs[...] = jnp.dot(h_scr[...], wup_ref[:, c0:c0 + wc], preferred_element_type=F32)
        val = jnp.dot(h_scr[HALO:HALO + tm, :], wup_ref[:, dff + c0:dff + c0 + wc], preferred_element_type=F32)
        cw = cw_ref[:, c0:c0 + wc]
        gc = (gs[HALO - 1:HALO - 1 + tm, :] * cw[0:1, :] + gs[HALO:HALO + tm, :] * cw[1:2, :]
              + gs[HALO + 1:HALO + 1 + tm, :] * cw[2:3, :] + cb_ref[:, c0:c0 + wc])
        act = (jax.nn.gelu(gc) * val).astype(BF16)
        y = y + jnp.dot(act, wdn_ref[c0:c0 + wc, :], preferred_element_type=F32)
        c0 += wc
    if final_norm:
        y = _rms(y, gf_ref[...])
    o_ref[0] = y


def _ffn(x, gain, w_up, conv_w, conv_b, w_down, final_gain, final_norm, tm):
    b, l, d = x.shape
    dff = w_down.shape[0]
    r = tm // HALO
    nh = l // HALO
    return pl.pallas_call(
        functools.partial(_ffn_kernel, final_norm=final_norm),
        grid=(b, l // tm),
        in_specs=[pl.BlockSpec((1, HALO, d), lambda i, j: (i, jnp.maximum(j * r - 1, 0), 0)),
                  pl.BlockSpec((1, tm, d), lambda i, j: (i, j, 0)),
                  pl.BlockSpec((1, HALO, d), lambda i, j: (i, jnp.minimum((j + 1) * r, nh - 1), 0)),
                  _const_spec((1, d)), _const_spec((d, 2 * dff)), _const_spec((3, dff)),
                  _const_spec((1, dff)), _const_spec((dff, d)), _const_spec((1, d))],
        out_specs=pl.BlockSpec((1, tm, d), lambda i, j: (i, j, 0)),
        out_shape=jax.ShapeDtypeStruct((b, l, d), F32),
        scratch_shapes=[pltpu.VMEM((tm + 2 * HALO, d), BF16),
                        pltpu.VMEM((tm + 2 * HALO, dff), F32)],
        compiler_params=_cparams("parallel", "arbitrary"),
        name="ffn",
    )(x, x, x, gain, w_up, conv_w, conv_b, w_down, final_gain)


def _rope_tables(l):
    rows = l // GRID_W
    r = jnp.broadcast_to(jnp.arange(rows, dtype=F32)[:, None], (rows, GRID_W)).reshape(l)
    c = jnp.broadcast_to(jnp.arange(GRID_W, dtype=F32)[None, :], (rows, GRID_W)).reshape(l)
    freqs = ROPE_THETA ** (-jnp.arange(ROPE_PAIRS, dtype=F32) / ROPE_PAIRS)
    ang_r, ang_c = r[:, None] * freqs, c[:, None] * freqs
    ang = jnp.concatenate([ang_r, ang_r, ang_c, ang_c], axis=-1)
    cos, sin = jnp.cos(ang), jnp.sin(ang)
    first = (jnp.arange(HEAD_DIM) % (2 * ROPE_PAIRS)) < ROPE_PAIRS
    sin_up = jnp.where(first, -sin, 0.0)
    sin_dn = jnp.where(first, 0.0, sin)
    two = lambda a: jnp.concatenate([a, a], axis=1)
    return two(cos), two(sin_up), two(sin_dn)


def _prepare_layer(p, l):
    bf = lambda a: a.astype(BF16)
    row = lambda a: a.astype(F32).reshape(1, -1)
    max_abs = lambda a: jnp.max(jnp.abs(a.astype(F32)))
    score_bound = 1.02 * QK_SCALE * HEAD_DIM * max_abs(p['q_norm'][l]) * max_abs(p['k_norm'][l])
    return dict(
        norm_mix=row(p['norm_mix'][l]), w_in=bf(p['w_in'][l]),
        q_gain=row(jnp.tile(p['q_norm'][l], N_HEADS)), k_gain=row(jnp.tile(p['k_norm'][l], N_KV_HEADS)),
        score_bound=score_bound,
        s5=_s5_operators(p['ssm_a_re'][l], p['ssm_a_im'][l], p['ssm_log_dt'][l], p['ssm_b_re'][l],
                         p['ssm_b_im'][l], p['ssm_c_re'][l], p['ssm_c_im'][l], S5_CHUNK),
        d_skip=row(p['ssm_d'][l]), w_glu=bf(p['ssm_glu'][l]),
        mem_norm=row(p['mem_norm'][l]), w_mem_kv=bf(p['w_mem_kv'][l]),
        p_attn=bf(p['p_attn'][l]), p_ssm=bf(p['p_ssm'][l]), p_cross=bf(p['p_cross'][l]), w_out=bf(p['w_out'][l]),
        norm_ffn=row(p['norm_ffn'][l]), w_up=bf(p['w_up'][l]), conv_w=p['conv_w'][l].astype(F32),
        conv_b=row(p['conv_b'][l]), w_down=bf(p['w_down'][l]),
    )


def _encode(x, mem, layers, final_gain):
    b, l, d = x.shape
    tables = _rope_tables(l)
    tk = _pick(l, 512)
    tq = _pick(l, 1024)
    tm = _pick(l, 512)
    for li, w in enumerate(layers):
        qt, k, vt, u, ut, qct, gates = _in_proj(x, w['norm_mix'], w['w_in'], tables, w['q_gain'], w['k_gain'], tk)
        attn = _attention(qt, k, vt, tq, w['score_bound'])
        yscan = _s5_scan_branch(ut, w['s5'])
        mk, mvt = _mem_kv(mem, w['mem_norm'], w['w_mem_kv'])
        x = _merge(x, attn, yscan, u, qct, gates, mk, mvt, w['d_skip'], w['w_glu'], w['p_attn'], w['p_ssm'],
                   w['p_cross'], w['w_out'], tm)
        x = _ffn(x, w['norm_ffn'], w['w_up'], w['conv_w'], w['conv_b'], w['w_down'], final_gain,
                 li == len(layers) - 1, tm)
    return x


def kernel(x_prompt, x_sample, mem_prompt, mem_sample, norm_mix, w_in, q_norm, k_norm, ssm_a_re, ssm_a_im, ssm_log_dt, ssm_b_re, ssm_b_im, ssm_c_re, ssm_c_im, ssm_d, ssm_glu, mem_norm, w_mem_kv, p_attn, p_ssm, p_cross, w_out, norm_ffn, w_up, conv_w, conv_b, w_down, norm_final):
    p = dict(norm_mix=norm_mix, w_in=w_in, q_norm=q_norm, k_norm=k_norm, ssm_a_re=ssm_a_re, ssm_a_im=ssm_a_im,
             ssm_log_dt=ssm_log_dt, ssm_b_re=ssm_b_re, ssm_b_im=ssm_b_im, ssm_c_re=ssm_c_re, ssm_c_im=ssm_c_im,
             ssm_d=ssm_d, ssm_glu=ssm_glu, mem_norm=mem_norm, w_mem_kv=w_mem_kv, p_attn=p_attn, p_ssm=p_ssm,
             p_cross=p_cross, w_out=w_out, norm_ffn=norm_ffn, w_up=w_up, conv_w=conv_w, conv_b=conv_b,
             w_down=w_down)
    layers = [_prepare_layer(p, l) for l in range(norm_mix.shape[0])]
    final_gain = norm_final.astype(F32).reshape(1, -1)
    y_prompt = _encode(x_prompt, mem_prompt, layers, final_gain)
    y_sample = _encode(x_sample, mem_sample, layers, final_gain)
    return (y_prompt, y_sample)
```

```python
import functools
import math

import jax
import jax.numpy as jnp
from jax import lax
from jax.experimental import pallas as pl
from jax.experimental.pallas import tpu as pltpu

HEAD_DIM = 64
N_HEADS = 8
N_KV_HEADS = 2
KV_GROUP = N_HEADS // N_KV_HEADS
ATTN_W = N_HEADS * HEAD_DIM
KV_W = N_KV_HEADS * HEAD_DIM
SSM_GROUP_CH = 16
SSM_W = 256
SSM_GROUPS = SSM_W // SSM_GROUP_CH
SSM_STATE = 64
N_CROSS_HEADS = 4
CROSS_W = N_CROSS_HEADS * HEAD_DIM
N_BRANCHES = 3
GRID_W = 64
ROPE_THETA = 10000.0
ROPE_PAIRS = HEAD_DIM // 4
EPS = 1e-6

LOG2E = 1.4426950408889634
QK_SCALE = HEAD_DIM ** -0.5 * LOG2E
SAFE_SCORE = 64.0
LANES = 128
S5_CHUNK = LANES
PV_ROWS = 2 * HEAD_DIM
HALO = 16
VMEM_LIMIT = 56 * 1024 * 1024

F32 = jnp.float32
BF16 = jnp.bfloat16


def _cparams(*sem):
    return pltpu.CompilerParams(dimension_semantics=sem, vmem_limit_bytes=VMEM_LIMIT)


def _const_spec(shape):
    nd = len(shape)
    return pl.BlockSpec(shape, lambda *_: (0,) * nd, pipeline_mode=pl.Buffered(1))


def _rms(x, gain):
    return x * lax.rsqrt(jnp.mean(x * x, axis=-1, keepdims=True) + EPS) * gain


def _sigmoid(x):
    return 0.5 * jnp.tanh(0.5 * x) + 0.5


def _pick(n, pref):
    t = min(n, pref)
    while n % t:
        t //= 2
    return t


def _head_norm_rope(x, gain, cos, sin_up, sin_dn, out_scale):
    w = x.shape[1]
    reps = w // LANES
    tile = lambda a: a if reps == 1 else jnp.concatenate([a] * reps, axis=1)
    xg = x * gain
    y = (xg * tile(cos)
         + pltpu.roll(xg, w - ROPE_PAIRS, 1) * tile(sin_up)
         + pltpu.roll(xg, ROPE_PAIRS, 1) * tile(sin_dn))
    lane = lax.broadcasted_iota(jnp.int32, (1, LANES), 1)
    low = lane < HEAD_DIM
    outs = []
    for p in range(reps):
        xp = x[:, p * LANES:(p + 1) * LANES]
        sq = xp * xp
        ss_all = jnp.sum(sq, axis=-1, keepdims=True)
        ss_lo = jnp.sum(jnp.where(low, sq, 0.0), axis=-1, keepdims=True)
        r_lo = lax.rsqrt(ss_lo * (1.0 / HEAD_DIM) + EPS) * out_scale
        r_hi = lax.rsqrt((ss_all - ss_lo) * (1.0 / HEAD_DIM) + EPS) * out_scale
        outs.append(y[:, p * LANES:(p + 1) * LANES] * jnp.where(low, r_lo, r_hi))
    return outs[0] if reps == 1 else jnp.concatenate(outs, axis=1)


def _in_proj_kernel(x_ref, g_ref, w_ref, cos_ref, sup_ref, sdn_ref, qg_ref, kg_ref,
                    qt_ref, k_ref, vt_ref, u_ref, ut_ref, qc_ref, gate_ref, h_scr):
    h_scr[...] = _rms(x_ref[0], g_ref[...]).astype(BF16)
    z = jnp.dot(h_scr[...], w_ref[...], preferred_element_type=F32)

    def seg(a, b):
        return z[:, a:b]

    cos, sup, sdn = cos_ref[...], sup_ref[...], sdn_ref[...]
    o = 0
    q = seg(o, o + ATTN_W)
    qt_ref[0] = _head_norm_rope(q, qg_ref[...], cos, sup, sdn, QK_SCALE).T.astype(BF16)
    o += ATTN_W
    kv = seg(o, o + 2 * KV_W)
    k_ref[0] = _head_norm_rope(kv[:, :KV_W], kg_ref[...], cos, sup, sdn, 1.0).astype(BF16)
    vt = kv[:, KV_W:].T.astype(BF16)
    ones = jnp.ones((PV_ROWS - HEAD_DIM, vt.shape[1]), BF16)
    for g in range(N_KV_HEADS):
        vt_ref[0, 0, g, :HEAD_DIM, :] = vt[g * HEAD_DIM:(g + 1) * HEAD_DIM, :]
        vt_ref[0, 0, g, HEAD_DIM:, :] = ones
    o += 2 * KV_W
    u = seg(o, o + SSM_W)
    u_ref[0] = u.astype(BF16)
    ut = u.T
    for c in range(ut_ref.shape[1]):
        ut_ref[0, c] = ut[:, c * S5_CHUNK:(c + 1) * S5_CHUNK]
    o += SSM_W
    qc_ref[0] = (seg(o, o + CROSS_W) * HEAD_DIM ** -0.5).T.astype(BF16)
    o += CROSS_W
    d = x_ref.shape[2]
    for j in range(N_BRANCHES):
        gate_ref[0, :, j * d:(j + 1) * d] = _sigmoid(seg(o + j * d, o + (j + 1) * d)).astype(BF16)


def _in_proj(x, gain, w, tables, q_gain, k_gain, tk):
    b, l, d = x.shape
    n_in = w.shape[1]
    nj = l // tk
    row = lambda width: pl.BlockSpec((1, tk, width), lambda i, j: (i, j, 0))
    tab = pl.BlockSpec((tk, LANES), lambda i, j: (j, 0))
    out_shape = (
        jax.ShapeDtypeStruct((b, ATTN_W, l), BF16),
        jax.ShapeDtypeStruct((b, l, KV_W), BF16),
        jax.ShapeDtypeStruct((b, nj, N_KV_HEADS, PV_ROWS, tk), BF16),
        jax.ShapeDtypeStruct((b, l, SSM_W), BF16),
        jax.ShapeDtypeStruct((b, l // S5_CHUNK, SSM_W, S5_CHUNK), F32),
        jax.ShapeDtypeStruct((b, CROSS_W, l), BF16),
        jax.ShapeDtypeStruct((b, l, N_BRANCHES * d), BF16),
    )
    col = lambda height: pl.BlockSpec((1, height, tk), lambda i, j: (i, 0, j))
    return pl.pallas_call(
        _in_proj_kernel,
        grid=(b, nj),
        in_specs=[row(d), _const_spec((1, d)), _const_spec((d, n_in)), tab, tab, tab,
                  _const_spec((1, ATTN_W)), _const_spec((1, KV_W))],
        out_specs=(col(ATTN_W), row(KV_W),
                   pl.BlockSpec((1, 1, N_KV_HEADS, PV_ROWS, tk), lambda i, j: (i, j, 0, 0, 0)),
                   row(SSM_W), pl.BlockSpec((1, tk // S5_CHUNK, SSM_W, S5_CHUNK), lambda i, j: (i, j, 0, 0)),
                   col(CROSS_W), row(N_BRANCHES * d)),
        out_shape=out_shape,
        scratch_shapes=[pltpu.VMEM((tk, d), BF16)],
        compiler_params=_cparams("parallel", "parallel"),
        name="in_proj",
    )(x, gain, w, *tables, q_gain, k_gain)


def _head_lanes(pair, h):
    return pair[:, (h % 2) * HEAD_DIM:(h % 2 + 1) * HEAD_DIM]


def _attn_kernel(qt_ref, k_ref, vt_ref, o_ref, qs_ref, acc_ref, m_ref, *, bounded):
    tq = qt_ref.shape[2]
    nc, tk = vt_ref.shape[1], vt_ref.shape[4]
    qs_ref[...] = jnp.zeros(qs_ref.shape, BF16)
    for h in range(N_HEADS):
        g, r = divmod(h, KV_GROUP)
        qs_ref[g, g * HEAD_DIM:(g + 1) * HEAD_DIM, r * tq:(r + 1) * tq] = qt_ref[0, h * HEAD_DIM:(h + 1) * HEAD_DIM, :]
    acc_ref[...] = jnp.zeros(acc_ref.shape, F32)
    if not bounded:
        m_ref[...] = jnp.full(m_ref.shape, -jnp.inf, F32)

    def body(c, carry):
        kc = k_ref[0, pl.ds(pl.multiple_of(c * tk, tk), tk), :]
        for g in range(N_KV_HEADS):
            st = jnp.dot(kc, qs_ref[g], preferred_element_type=F32)
            if bounded:
                pt = jnp.exp2(st).astype(BF16)
                acc_ref[g] += jnp.dot(vt_ref[0, c, g], pt, preferred_element_type=F32)
            else:
                m_old = m_ref[g]
                m_new = jnp.maximum(m_old, jnp.max(st, axis=0, keepdims=True))
                pt = jnp.exp2(st - m_new).astype(BF16)
                acc_ref[g] = (jnp.exp2(m_old - m_new) * acc_ref[g]
                              + jnp.dot(vt_ref[0, c, g], pt, preferred_element_type=F32))
                m_ref[g] = m_new
        return carry

    lax.fori_loop(0, nc, body, 0, unroll=4)
    for h in range(0, N_HEADS, 2):
        g, r = divmod(h, KV_GROUP)
        acc = acc_ref[g, :, r * tq:(r + 2) * tq]
        ot = acc[:HEAD_DIM] / acc[HEAD_DIM:HEAD_DIM + 1]
        pair = jnp.concatenate([ot[:, :tq], ot[:, tq:]], axis=0)
        o_ref[0, :, h * HEAD_DIM:(h + 2) * HEAD_DIM] = pair.T.astype(BF16)


def _attention(qt, k, vt, tq, score_bound):
    b, _, l = qt.shape
    nc, tk = vt.shape[1], vt.shape[4]

    def call(bounded):
        return pl.pallas_call(
            functools.partial(_attn_kernel, bounded=bounded),
            grid=(b, l // tq),
            in_specs=[pl.BlockSpec((1, ATTN_W, tq), lambda i, j: (i, 0, j)),
                      pl.BlockSpec((1, l, KV_W), lambda i, j: (i, 0, 0)),
                      pl.BlockSpec((1, nc, N_KV_HEADS, PV_ROWS, tk), lambda i, j: (i, 0, 0, 0, 0))],
            out_specs=pl.BlockSpec((1, tq, ATTN_W), lambda i, j: (i, j, 0)),
            out_shape=jax.ShapeDtypeStruct((b, l, ATTN_W), BF16),
            scratch_shapes=[pltpu.VMEM((N_KV_HEADS, KV_W, KV_GROUP * tq), BF16),
                            pltpu.VMEM((N_KV_HEADS, PV_ROWS, KV_GROUP * tq), F32),
                            pltpu.VMEM((N_KV_HEADS, 1, KV_GROUP * tq), F32)],
            compiler_params=_cparams("parallel", "parallel"),
            name="attention" if bounded else "attention_running_max",
        )

    return lax.cond(score_bound <= SAFE_SCORE, call(True), call(False), qt, k, vt)


def _complex_powers(lr, li, n):
    pr, pi = jnp.ones_like(lr)[None], jnp.zeros_like(li)[None]
    cr, ci = lr, li
    while pr.shape[0] < n:
        pr, pi = (jnp.concatenate([pr, pr * cr - pi * ci]), jnp.concatenate([pi, pr * ci + pi * cr]))
        cr, ci = cr * cr - ci * ci, 2.0 * cr * ci
    return pr[:n], pi[:n]


def _toeplitz_kernel(k_ref, o_ref):
    n, hch, t = k_ref.shape[1], k_ref.shape[2], k_ref.shape[3] // 2
    for i in range(n):
        for h in range(hch):
            rows = jnp.broadcast_to(k_ref[0, i, h:h + 1, :], (t, 2 * t))
            rolled = pltpu.roll(rows, t + 1, 1, stride=1, stride_axis=0)
            o_ref[0, i * t:(i + 1) * t, h * t:(h + 1) * t] = rolled[:, :t].astype(BF16)


def _toeplitz(k_cat):
    g, hin, hout, t2 = k_cat.shape
    t = t2 // 2
    n = _pick(hin, 4)
    return pl.pallas_call(
        _toeplitz_kernel,
        grid=(g, hin // n),
        in_specs=[pl.BlockSpec((1, n, hout, t2), lambda i, j: (i, j, 0, 0))],
        out_specs=pl.BlockSpec((1, n * t, hout * t), lambda i, j: (i, j, 0)),
        out_shape=jax.ShapeDtypeStruct((g, hin * t, hout * t), BF16),
        compiler_params=_cparams("parallel", "parallel"),
        name="s5_toeplitz",
    )(k_cat)


def _s5_operators(a_re, a_im, log_dt, b_re, b_im, c_re, c_im, t):
    hi = lax.Precision.HIGHEST
    a_re, a_im, log_dt = a_re.astype(F32), a_im.astype(F32), log_dt.astype(F32)
    b_re, b_im, c_re, c_im = (z.astype(F32) for z in (b_re, b_im, c_re, c_im))
    dt = jnp.exp(log_dt)[..., None]
    mag = jnp.exp(a_re * dt)
    lam_re, lam_im = mag * jnp.cos(a_im * dt), mag * jnp.sin(a_im * dt)
    num_re = lam_re - 1.0
    den = a_re * a_re + a_im * a_im
    coef_re = (num_re * a_re + lam_im * a_im) / den
    coef_im = (lam_im * a_re - num_re * a_im) / den
    pw_re, pw_im = _complex_powers(lam_re, lam_im, t + 1)
    e_re = pw_re * coef_re - pw_im * coef_im
    e_im = pw_re * coef_im + pw_im * coef_re
    g, p, hch = b_re.shape

    cb_re = c_re[:, :, :, None] * b_re[:, None, :, :] - c_im[:, :, :, None] * b_im[:, None, :, :]
    cb_im = c_re[:, :, :, None] * b_im[:, None, :, :] + c_im[:, :, :, None] * b_re[:, None, :, :]
    kern = (jnp.einsum('tdgp,ghpk->dgthk', e_re[:t], cb_re, precision=hi)
            - jnp.einsum('tdgp,ghpk->dgthk', e_im[:t], cb_im, precision=hi))
    k_f, k_b = kern[0], kern[1]
    k_cat = jnp.concatenate([k_b[:, :0:-1], (k_f[:, :1] + k_b[:, :1]), k_f[:, 1:]], axis=1)
    k_cat = jnp.pad(k_cat.transpose(0, 3, 2, 1), ((0, 0), (0, 0), (0, 0), (0, 1)))
    toep = _toeplitz(k_cat)

    bt_re, bt_im = b_re.transpose(0, 2, 1)[:, :, None, :], b_im.transpose(0, 2, 1)[:, :, None, :]

    def times_b(er, ei):
        er, ei = er.transpose(1, 0, 2)[:, None], ei.transpose(1, 0, 2)[:, None]
        re = er * bt_re - ei * bt_im
        im = er * bt_im + ei * bt_re
        f = lambda z: z.reshape(g, hch * t, p)
        return f(re), f(im)
    sf_re, sf_im = times_b(e_re[:t, 0][::-1], e_im[:t, 0][::-1])
    sb_re, sb_im = times_b(e_re[:t, 1], e_im[:t, 1])
    w_s = jnp.concatenate([sf_re, sf_im, sb_re, sb_im], axis=2)

    ct_re, ct_im = c_re.transpose(0, 2, 1)[:, :, :, None], c_im.transpose(0, 2, 1)[:, :, :, None]

    def c_times(pr, pi):
        pr, pi = pr.transpose(1, 2, 0)[:, :, None, :], pi.transpose(1, 2, 0)[:, :, None, :]
        re = ct_re * pr - ct_im * pi
        im = ct_re * pi + ct_im * pr
        f = lambda z: z.reshape(g, p, hch * t)
        return f(re), f(-im)
    of_re, of_im = c_times(pw_re[1:t + 1, 0], pw_im[1:t + 1, 0])
    ob_re, ob_im = c_times(pw_re[1:t + 1, 1][::-1], pw_im[1:t + 1, 1][::-1])
    w_o = jnp.concatenate([of_re, of_im, ob_re, ob_im], axis=1)

    lt_re, lt_im = pw_re[t], pw_im[t]
    a1 = jnp.concatenate([lt_re[0], lt_re[0], lt_re[1], lt_re[1]], axis=-1)[:, None, :]
    a2 = jnp.concatenate([-lt_im[0], lt_im[0], -lt_im[1], lt_im[1]], axis=-1)[:, None, :]
    return toep.astype(BF16), w_s.astype(BF16), w_o.astype(BF16), a1, a2


def _s5_chunk_rows(u_ref):
    return jnp.concatenate([u_ref[:, h, :] for h in range(u_ref.shape[1])], axis=1).astype(BF16)


def _s5_state_kernel(u_ref, ws_ref, s_ref):
    s_ref[0] = jnp.dot(_s5_chunk_rows(u_ref), ws_ref[0], preferred_element_type=F32)


def _s5_state(ut, w_s):
    r, ch, t = ut.shape
    g, k, n = w_s.shape
    return pl.pallas_call(
        _s5_state_kernel,
        grid=(g,),
        in_specs=[pl.BlockSpec((r, ch // g, t), lambda i: (0, i, 0)),
                  pl.BlockSpec((1, k, n), lambda i: (i, 0, 0))],
        out_specs=pl.BlockSpec((1, r, n), lambda i: (i, 0, 0)),
        out_shape=jax.ShapeDtypeStruct((g, r, n), F32),
        compiler_params=_cparams("parallel"),
        name="s5_state",
    )(ut, w_s)


def _s5_scan_kernel(s_ref, a1_ref, a2_ref, h_ref, *, nb):
    gb, rows, _ = s_ref.shape
    nchunk = rows // (2 * nb)
    half = 2 * SSM_STATE
    a1, a2 = a1_ref[...], a2_ref[...]
    a1f, a1b, a2f, a2b = a1[:, :, :half], a1[:, :, half:], a2[:, :, :half], a2[:, :, half:]

    def swap(z):
        return pltpu.roll(z.reshape(gb * nb, half), SSM_STATE, 1).reshape(gb, nb, half)

    def body(j, carry):
        hf, hb = carry
        rf = pl.ds(2 * j, nb, stride=2 * nchunk)
        rb = pl.ds(2 * (nchunk - 1 - j) + 1, nb, stride=2 * nchunk)
        h_ref[:, rf, :] = hf
        h_ref[:, rb, :] = hb
        hf = a1f * hf + a2f * swap(hf) + s_ref[:, rf, :]
        hb = a1b * hb + a2b * swap(hb) + s_ref[:, rb, :]
        return hf, hb

    zero = jnp.zeros((gb, nb, half), F32)
    lax.fori_loop(0, nchunk, body, (zero, zero))


def _s5_scan(s, a1, a2, nb, gb):
    g, r, n = s.shape
    half = n // 2
    blk = pl.BlockSpec((gb, 2 * r, half), lambda i: (i, 0, 0))
    par = pl.BlockSpec((gb, 1, n), lambda i: (i, 0, 0))
    return pl.pallas_call(
        functools.partial(_s5_scan_kernel, nb=nb),
        grid=(g // gb,),
        in_specs=[blk, par, par],
        out_specs=blk,
        out_shape=jax.ShapeDtypeStruct((g, 2 * r, half), F32),
        compiler_params=_cparams("parallel"),
        name="s5_scan",
    )(s.reshape(g, 2 * r, half), a1, a2).reshape(g, r, n)


def _s5_out_kernel(u_ref, h_ref, toep_ref, wo_ref, y_ref):
    _, hch, t = u_ref.shape
    y = jnp.dot(_s5_chunk_rows(u_ref), toep_ref[0], preferred_element_type=F32)
    y = y + jnp.dot(h_ref[0].astype(BF16), wo_ref[0], preferred_element_type=F32)
    for h in range(hch):
        y_ref[:, h, :] = y[:, h * t:(h + 1) * t]


def _s5_out(ut, hin, toep, w_o):
    r, ch, t = ut.shape
    g, n, k = w_o.shape
    blk = pl.BlockSpec((r, ch // g, t), lambda i: (0, i, 0))
    return pl.pallas_call(
        _s5_out_kernel,
        grid=(g,),
        in_specs=[blk, pl.BlockSpec((1, r, n), lambda i: (i, 0, 0)),
                  pl.BlockSpec((1, k, k), lambda i: (i, 0, 0)),
                  pl.BlockSpec((1, n, k), lambda i: (i, 0, 0))],
        out_specs=blk,
        out_shape=jax.ShapeDtypeStruct((r, ch, t), F32),
        compiler_params=_cparams("parallel"),
        name="s5_out",
    )(ut, hin, toep, w_o)


def _s5_scan_branch(ut, ops):
    toep, w_s, w_o, a1, a2 = ops
    b, c, ch, t = ut.shape
    ut = ut.reshape(b * c, ch, t)
    s = _s5_state(ut, w_s)
    hin = _s5_scan(s, a1, a2, b, 4)
    return _s5_out(ut, hin, toep, w_o).reshape(b, c, ch, t)


def _mem_kv_kernel(m_ref, g_ref, w_ref, mk_ref, mvt_ref):
    h = _rms(m_ref[0], g_ref[...]).astype(BF16)
    kv = jnp.dot(h, w_ref[...], preferred_element_type=F32)
    mk_ref[0] = kv[:, :CROSS_W].astype(BF16)
    vt = kv[:, CROSS_W:].T.astype(BF16)
    ones = jnp.ones((PV_ROWS - HEAD_DIM, vt.shape[1]), BF16)
    for hd in range(N_CROSS_HEADS):
        mvt_ref[0, hd, :HEAD_DIM, :] = vt[hd * HEAD_DIM:(hd + 1) * HEAD_DIM, :]
        mvt_ref[0, hd, HEAD_DIM:, :] = ones


def _mem_kv(mem, gain, w):
    b, m, d = mem.shape
    n = w.shape[1]
    return pl.pallas_call(
        _mem_kv_kernel,
        grid=(b,),
        in_specs=[pl.BlockSpec((1, m, d), lambda i: (i, 0, 0)), _const_spec((1, d)), _const_spec((d, n))],
        out_specs=(pl.BlockSpec((1, m, CROSS_W), lambda i: (i, 0, 0)),
                   pl.BlockSpec((1, N_CROSS_HEADS, PV_ROWS, m), lambda i: (i, 0, 0, 0))),
        out_shape=(jax.ShapeDtypeStruct((b, m, CROSS_W), BF16),
                   jax.ShapeDtypeStruct((b, N_CROSS_HEADS, PV_ROWS, m), BF16)),
        compiler_params=_cparams("parallel"),
        name="mem_kv",
    )(mem, gain, w)


def _merge_kernel(x_ref, a_ref, y_ref, u_ref, qct_ref, gate_ref, mk_ref, mvt_ref, dskip_ref, wglu_ref,
                  pa_ref, ps_ref, pc_ref, wout_ref, o_ref):
    d = x_ref.shape[2]
    heads = []
    for h in range(N_CROSS_HEADS):
        mk = _head_lanes(mk_ref[0, :, (h // 2) * LANES:(h // 2 + 1) * LANES], h)
        st = jnp.dot(mk, qct_ref[0, h * HEAD_DIM:(h + 1) * HEAD_DIM, :], preferred_element_type=F32)
        pt = jnp.exp(st - jnp.max(st, axis=0, keepdims=True)).astype(BF16)
        ot = jnp.dot(mvt_ref[0, h], pt, preferred_element_type=F32)
        heads.append(ot[:HEAD_DIM] / ot[HEAD_DIM:HEAD_DIM + 1])
    cross = jnp.concatenate(heads, axis=0).T.astype(BF16)
    y = jnp.concatenate([y_ref[0, c].T for c in range(y_ref.shape[1])], axis=0)
    y = y + dskip_ref[...] * u_ref[0].astype(F32)
    y = jax.nn.gelu(y)
    y = y * _sigmoid(jnp.dot(y.astype(BF16), wglu_ref[...], preferred_element_type=F32))
    gate = lambda j: gate_ref[0, :, j * d:(j + 1) * d].astype(F32)
    m = gate(0) * jnp.dot(a_ref[0], pa_ref[...], preferred_element_type=F32)
    m = m + gate(1) * jnp.dot(y.astype(BF16), ps_ref[...], preferred_element_type=F32)
    m = m + gate(2) * jnp.dot(cross, pc_ref[...], preferred_element_type=F32)
    o_ref[0] = x_ref[0] + jnp.dot(m.astype(BF16), wout_ref[...], preferred_element_type=F32)


def _merge(x, attn, yscan, u, qct, gates, mk, mvt, d_skip, w_glu, p_attn, p_ssm, p_cross, w_out, tl):
    b, l, d = x.shape
    row = lambda width: pl.BlockSpec((1, tl, width), lambda i, j: (i, j, 0))
    col = lambda height: pl.BlockSpec((1, height, tl), lambda i, j: (i, 0, j))
    n_mem = mk.shape[1]
    return pl.pallas_call(
        _merge_kernel,
        grid=(b, l // tl),
        in_specs=[row(d), row(ATTN_W),
                  pl.BlockSpec((1, tl // S5_CHUNK, SSM_W, S5_CHUNK), lambda i, j: (i, j, 0, 0)),
                  row(SSM_W), col(CROSS_W), row(N_BRANCHES * d),
                  pl.BlockSpec((1, n_mem, CROSS_W), lambda i, j: (i, 0, 0)),
                  pl.BlockSpec((1, N_CROSS_HEADS, PV_ROWS, n_mem), lambda i, j: (i, 0, 0, 0)),
                  _const_spec((1, SSM_W)), _const_spec((SSM_W, SSM_W)),
                  _const_spec((ATTN_W, d)), _const_spec((SSM_W, d)), _const_spec((CROSS_W, d)),
                  _const_spec((d, d))],
        out_specs=row(d),
        out_shape=jax.ShapeDtypeStruct((b, l, d), F32),
        compiler_params=_cparams("parallel", "parallel"),
        name="merge",
    )(x, attn, yscan, u, qct, gates, mk, mvt, d_skip, w_glu, p_attn, p_ssm, p_cross, w_out)


MXU_TILE = 256


def _ffn_chunks(dff, tiles_per_chunk=11):
    step = tiles_per_chunk * MXU_TILE
    return [min(step, dff - c0) for c0 in range(0, dff, step)]


def _ffn_kernel(xp_ref, x_ref, xn_ref, g_ref, wup_ref, cw_ref, cb_ref, wdn_ref, gf_ref, o_ref,
                h_scr, gate_scr, *, final_norm):
    j = pl.program_id(1)
    nj = pl.num_programs(1)
    tm = x_ref.shape[1]
    dff = wdn_ref.shape[0]
    gain = g_ref[...]
    x = x_ref[0]
    h_scr[0:HALO, :] = jnp.where(j == 0, 0.0, _rms(xp_ref[0], gain)).astype(BF16)
    h_scr[HALO:HALO + tm, :] = _rms(x, gain).astype(BF16)
    h_scr[HALO + tm:, :] = jnp.where(j == nj - 1, 0.0, _rms(xn_ref[0], gain)).astype(BF16)
    y = x
    c0 = 0
    for wc in _ffn_chunks(dff):
        gs = gate_scr.at[:, c0:c0 + wc]
        gs[...] = jnp.dot(h_scr[...], wup_ref[:, c0:c0 + wc], preferred_element_type=F32)
        val = jnp.dot(h_scr[HALO:HALO + tm, :], wup_ref[:, dff + c0:dff + c0 + wc], preferred_element_type=F32)
        cw = cw_ref[:, c0:c0 + wc]
        gc = (gs[HALO - 1:HALO - 1 + tm, :] * cw[0:1, :] + gs[HALO:HALO + tm, :] * cw[1:2, :]
              + gs[HALO + 1:HALO + 1 + tm, :] * cw[2:3, :] + cb_ref[:, c0:c0 + wc])
        act = (jax.nn.gelu(gc) * val).astype(BF16)
        y = y + jnp.dot(act, wdn_ref[c0:c0 + wc, :], preferred_element_type=F32)
        c0 += wc
    if final_norm:
        y = _rms(y, gf_ref[...])
    o_ref[0] = y


def _ffn(x, gain, w_up, conv_w, conv_b, w_down, final_gain, final_norm, tm):
    b, l, d = x.shape
    dff = w_down.shape[0]
    r = tm // HALO
    nh = l // HALO
    return pl.pallas_call(
        functools.partial(_ffn_kernel, final_norm=final_norm),
        grid=(b, l // tm),
        in_specs=[pl.BlockSpec((1, HALO, d), lambda i, j: (i, jnp.maximum(j * r - 1, 0), 0)),
                  pl.BlockSpec((1, tm, d), lambda i, j: (i, j, 0)),
                  pl.BlockSpec((1, HALO, d), lambda i, j: (i, jnp.minimum((j + 1) * r, nh - 1), 0)),
                  _const_spec((1, d)), _const_spec((d, 2 * dff)), _const_spec((3, dff)),
                  _const_spec((1, dff)), _const_spec((dff, d)), _const_spec((1, d))],
        out_specs=pl.BlockSpec((1, tm, d), lambda i, j: (i, j, 0)),
        out_shape=jax.ShapeDtypeStruct((b, l, d), F32),
        scratch_shapes=[pltpu.VMEM((tm + 2 * HALO, d), BF16),
                        pltpu.VMEM((tm + 2 * HALO, dff), F32)],
        compiler_params=_cparams("parallel", "arbitrary"),
        name="ffn",
    )(x, x, x, gain, w_up, conv_w, conv_b, w_down, final_gain)


def _rope_tables(l):
    rows = l // GRID_W
    r = jnp.broadcast_to(jnp.arange(rows, dtype=F32)[:, None], (rows, GRID_W)).reshape(l)
    c = jnp.broadcast_to(jnp.arange(GRID_W, dtype=F32)[None, :], (rows, GRID_W)).reshape(l)
    freqs = ROPE_THETA ** (-jnp.arange(ROPE_PAIRS, dtype=F32) / ROPE_PAIRS)
    ang_r, ang_c = r[:, None] * freqs, c[:, None] * freqs
    ang = jnp.concatenate([ang_r, ang_r, ang_c, ang_c], axis=-1)
    cos, sin = jnp.cos(ang), jnp.sin(ang)
    first = (jnp.arange(HEAD_DIM) % (2 * ROPE_PAIRS)) < ROPE_PAIRS
    sin_up = jnp.where(first, -sin, 0.0)
    sin_dn = jnp.where(first, 0.0, sin)
    two = lambda a: jnp.concatenate([a, a], axis=1)
    return two(cos), two(sin_up), two(sin_dn)


def _prepare_layer(p, l):
    bf = lambda a: a.astype(BF16)
    row = lambda a: a.astype(F32).reshape(1, -1)
    max_abs = lambda a: jnp.max(jnp.abs(a.astype(F32)))
    score_bound = 1.02 * QK_SCALE * HEAD_DIM * max_abs(p['q_norm'][l]) * max_abs(p['k_norm'][l])
    return dict(
        norm_mix=row(p['norm_mix'][l]), w_in=bf(p['w_in'][l]),
        q_gain=row(jnp.tile(p['q_norm'][l], N_HEADS)), k_gain=row(jnp.tile(p['k_norm'][l], N_KV_HEADS)),
        score_bound=score_bound,
        s5=_s5_operators(p['ssm_a_re'][l], p['ssm_a_im'][l], p['ssm_log_dt'][l], p['ssm_b_re'][l],
                         p['ssm_b_im'][l], p['ssm_c_re'][l], p['ssm_c_im'][l], S5_CHUNK),
        d_skip=row(p['ssm_d'][l]), w_glu=bf(p['ssm_glu'][l]),
        mem_norm=row(p['mem_norm'][l]), w_mem_kv=bf(p['w_mem_kv'][l]),
        p_attn=bf(p['p_attn'][l]), p_ssm=bf(p['p_ssm'][l]), p_cross=bf(p['p_cross'][l]), w_out=bf(p['w_out'][l]),
        norm_ffn=row(p['norm_ffn'][l]), w_up=bf(p['w_up'][l]), conv_w=p['conv_w'][l].astype(F32),
        conv_b=row(p['conv_b'][l]), w_down=bf(p['w_down'][l]),
    )


def _encode(x, mem, layers, final_gain):
    b, l, d = x.shape
    tables = _rope_tables(l)
    tk = _pick(l, 512)
    tq = _pick(l, 1024)
    tm = _pick(l, 512)
    for li, w in enumerate(layers):
        qt, k, vt, u, ut, qct, gates = _in_proj(x, w['norm_mix'], w['w_in'], tables, w['q_gain'], w['k_gain'], tk)
        attn = _attention(qt, k, vt, tq, w['score_bound'])
        yscan = _s5_scan_branch(ut, w['s5'])
        mk, mvt = _mem_kv(mem, w['mem_norm'], w['w_mem_kv'])
        x = _merge(x, attn, yscan, u, qct, gates, mk, mvt, w['d_skip'], w['w_glu'], w['p_attn'], w['p_ssm'],
                   w['p_cross'], w['w_out'], tm)
        x = _ffn(x, w['norm_ffn'], w['w_up'], w['conv_w'], w['conv_b'], w['w_down'], final_gain,
                 li == len(layers) - 1, tm)
    return x


def kernel(x_prompt, x_sample, mem_prompt, mem_sample, norm_mix, w_in, q_norm, k_norm, ssm_a_re, ssm_a_im, ssm_log_dt, ssm_b_re, ssm_b_im, ssm_c_re, ssm_c_im, ssm_d, ssm_glu, mem_norm, w_mem_kv, p_attn, p_ssm, p_cross, w_out, norm_ffn, w_up, conv_w, conv_b, w_down, norm_final):
    p = dict(norm_mix=norm_mix, w_in=w_in, q_norm=q_norm, k_norm=k_norm, ssm_a_re=ssm_a_re, ssm_a_im=ssm_a_im,
             ssm_log_dt=ssm_log_dt, ssm_b_re=ssm_b_re, ssm_b_im=ssm_b_im, ssm_c_re=ssm_c_re, ssm_c_im=ssm_c_im,
             ssm_d=ssm_d, ssm_glu=ssm_glu, mem_norm=mem_norm, w_mem_kv=w_mem_kv, p_attn=p_attn, p_ssm=p_ssm,
             p_cross=p_cross, w_out=w_out, norm_ffn=norm_ffn, w_up=w_up, conv_w=conv_w, conv_b=conv_b,
             w_down=w_down)
    layers = [_prepare_layer(p, l) for l in range(norm_mix.shape[0])]
    final_gain = norm_final.astype(F32).reshape(1, -1)
    y_prompt = _encode(x_prompt, mem_prompt, layers, final_gain)
    y_sample = _encode(x_sample, mem_sample, layers, final_gain)
    return (y_prompt, y_sample)
```

```python
import functools
import math

import jax
import jax.numpy as jnp
from jax import lax
from jax.experimental import pallas as pl
from jax.experimental.pallas import tpu as pltpu

HEAD_DIM = 64
N_HEADS = 8
N_KV_HEADS = 2
KV_GROUP = N_HEADS // N_KV_HEADS
ATTN_W = N_HEADS * HEAD_DIM
KV_W = N_KV_HEADS * HEAD_DIM
SSM_GROUP_CH = 16
SSM_W = 256
SSM_GROUPS = SSM_W // SSM_GROUP_CH
SSM_STATE = 64
N_CROSS_HEADS = 4
CROSS_W = N_CROSS_HEADS * HEAD_DIM
N_BRANCHES = 3
GRID_W = 64
ROPE_THETA = 10000.0
ROPE_PAIRS = HEAD_DIM // 4
EPS = 1e-6

LOG2E = 1.4426950408889634
QK_SCALE = HEAD_DIM ** -0.5 * LOG2E
SAFE_SCORE = 64.0
LANES = 128
S5_CHUNK = LANES
PV_ROWS = 2 * HEAD_DIM
HALO = 16
VMEM_LIMIT = 56 * 1024 * 1024

F32 = jnp.float32
BF16 = jnp.bfloat16


def _cparams(*sem):
    return pltpu.CompilerParams(dimension_semantics=sem, vmem_limit_bytes=VMEM_LIMIT)


def _const_spec(shape):
    nd = len(shape)
    return pl.BlockSpec(shape, lambda *_: (0,) * nd, pipeline_mode=pl.Buffered(1))


def _rms(x, gain):
    return x * lax.rsqrt(jnp.mean(x * x, axis=-1, keepdims=True) + EPS) * gain


def _sigmoid(x):
    return 0.5 * jnp.tanh(0.5 * x) + 0.5


def _pick(n, pref):
    t = min(n, pref)
    while n % t:
        t //= 2
    return t


def _head_norm_rope(x, gain, cos, sin_up, sin_dn, out_scale):
    w = x.shape[1]
    reps = w // LANES
    tile = lambda a: a if reps == 1 else jnp.concatenate([a] * reps, axis=1)
    xg = x * gain
    y = (xg * tile(cos)
         + pltpu.roll(xg, w - ROPE_PAIRS, 1) * tile(sin_up)
         + pltpu.roll(xg, ROPE_PAIRS, 1) * tile(sin_dn))
    lane = lax.broadcasted_iota(jnp.int32, (1, LANES), 1)
    low = lane < HEAD_DIM
    outs = []
    for p in range(reps):
        xp = x[:, p * LANES:(p + 1) * LANES]
        sq = xp * xp
        ss_all = jnp.sum(sq, axis=-1, keepdims=True)
        ss_lo = jnp.sum(jnp.where(low, sq, 0.0), axis=-1, keepdims=True)
        r_lo = lax.rsqrt(ss_lo * (1.0 / HEAD_DIM) + EPS) * out_scale
        r_hi = lax.rsqrt((ss_all - ss_lo) * (1.0 / HEAD_DIM) + EPS) * out_scale
        outs.append(y[:, p * LANES:(p + 1) * LANES] * jnp.where(low, r_lo, r_hi))
    return outs[0] if reps == 1 else jnp.concatenate(outs, axis=1)


def _in_proj_kernel(x_ref, g_ref, w_ref, cos_ref, sup_ref, sdn_ref, qg_ref, kg_ref,
                    qt_ref, k_ref, vt_ref, u_ref, ut_ref, qc_ref, gate_ref, h_scr):
    h_scr[...] = _rms(x_ref[0], g_ref[...]).astype(BF16)
    z = jnp.dot(h_scr[...], w_ref[...], preferred_element_type=F32)

    def seg(a, b):
        return z[:, a:b]

    cos, sup, sdn = cos_ref[...], sup_ref[...], sdn_ref[...]
    o = 0
    q = seg(o, o + ATTN_W)
    qt_ref[0] = _head_norm_rope(q, qg_ref[...], cos, sup, sdn, QK_SCALE).T.astype(BF16)
    o += ATTN_W
    kv = seg(o, o + 2 * KV_W)
    k_ref[0] = _head_norm_rope(kv[:, :KV_W], kg_ref[...], cos, sup, sdn, 1.0).astype(BF16)
    vt = kv[:, KV_W:].T.astype(BF16)
    ones = jnp.ones((PV_ROWS - HEAD_DIM, vt.shape[1]), BF16)
    for g in range(N_KV_HEADS):
        vt_ref[0, 0, g, :HEAD_DIM, :] = vt[g * HEAD_DIM:(g + 1) * HEAD_DIM, :]
        vt_ref[0, 0, g, HEAD_DIM:, :] = ones
    o += 2 * KV_W
    u = seg(o, o + SSM_W)
    u_ref[0] = u.astype(BF16)
    ut = u.T
    for c in range(ut_ref.shape[1]):
        ut_ref[0, c] = ut[:, c * S5_CHUNK:(c + 1) * S5_CHUNK]
    o += SSM_W
    qc_ref[0] = (seg(o, o + CROSS_W) * HEAD_DIM ** -0.5).T.astype(BF16)
    o += CROSS_W
    d = x_ref.shape[2]
    for j in range(N_BRANCHES):
        gate_ref[0, :, j * d:(j + 1) * d] = _sigmoid(seg(o + j * d, o + (j + 1) * d)).astype(BF16)


def _in_proj(x, gain, w, tables, q_gain, k_gain, tk):
    b, l, d = x.shape
    n_in = w.shape[1]
    nj = l // tk
    row = lambda width: pl.BlockSpec((1, tk, width), lambda i, j: (i, j, 0))
    tab = pl.BlockSpec((tk, LANES), lambda i, j: (j, 0))
    out_shape = (
        jax.ShapeDtypeStruct((b, ATTN_W, l), BF16),
        jax.ShapeDtypeStruct((b, l, KV_W), BF16),
        jax.ShapeDtypeStruct((b, nj, N_KV_HEADS, PV_ROWS, tk), BF16),
        jax.ShapeDtypeStruct((b, l, SSM_W), BF16),
        jax.ShapeDtypeStruct((b, l // S5_CHUNK, SSM_W, S5_CHUNK), F32),
        jax.ShapeDtypeStruct((b, CROSS_W, l), BF16),
        jax.ShapeDtypeStruct((b, l, N_BRANCHES * d), BF16),
    )
    col = lambda height: pl.BlockSpec((1, height, tk), lambda i, j: (i, 0, j))
    return pl.pallas_call(
        _in_proj_kernel,
        grid=(b, nj),
        in_specs=[row(d), _const_spec((1, d)), _const_spec((d, n_in)), tab, tab, tab,
                  _const_spec((1, ATTN_W)), _const_spec((1, KV_W))],
        out_specs=(col(ATTN_W), row(KV_W),
                   pl.BlockSpec((1, 1, N_KV_HEADS, PV_ROWS, tk), lambda i, j: (i, j, 0, 0, 0)),
                   row(SSM_W), pl.BlockSpec((1, tk // S5_CHUNK, SSM_W, S5_CHUNK), lambda i, j: (i, j, 0, 0)),
                   col(CROSS_W), row(N_BRANCHES * d)),
        out_shape=out_shape,
        scratch_shapes=[pltpu.VMEM((tk, d), BF16)],
        compiler_params=_cparams("parallel", "parallel"),
        name="in_proj",
    )(x, gain, w, *tables, q_gain, k_gain)


def _head_lanes(pair, h):
    return pair[:, (h % 2) * HEAD_DIM:(h % 2 + 1) * HEAD_DIM]


def _attn_kernel(qt_ref, k_ref, vt_ref, o_ref, qs_ref, acc_ref, m_ref, *, bounded):
    tq = qt_ref.shape[2]
    nc, tk = vt_ref.shape[1], vt_ref.shape[4]
    qs_ref[...] = jnp.zeros(qs_ref.shape, BF16)
    for h in range(N_HEADS):
        g, r = divmod(h, KV_GROUP)
        qs_ref[g, g * HEAD_DIM:(g + 1) * HEAD_DIM, r * tq:(r + 1) * tq] = qt_ref[0, h * HEAD_DIM:(h + 1) * HEAD_DIM, :]
    acc_ref[...] = jnp.zeros(acc_ref.shape, F32)
    if not bounded:
        m_ref[...] = jnp.full(m_ref.shape, -jnp.inf, F32)

    def body(c, carry):
        kc = k_ref[0, pl.ds(pl.multiple_of(c * tk, tk), tk), :]
        for g in range(N_KV_HEADS):
            st = jnp.dot(kc, qs_ref[g], preferred_element_type=F32)
            if bounded:
                pt = jnp.exp2(st).astype(BF16)
                acc_ref[g] += jnp.dot(vt_ref[0, c, g], pt, preferred_element_type=F32)
            else:
                m_old = m_ref[g]
                m_new = jnp.maximum(m_old, jnp.max(st, axis=0, keepdims=True))
                pt = jnp.exp2(st - m_new).astype(BF16)
                acc_ref[g] = (jnp.exp2(m_old - m_new) * acc_ref[g]
                              + jnp.dot(vt_ref[0, c, g], pt, preferred_element_type=F32))
                m_ref[g] = m_new
        return carry

    lax.fori_loop(0, nc, body, 0, unroll=4)
    for h in range(0, N_HEADS, 2):
        g, r = divmod(h, KV_GROUP)
        acc = acc_ref[g, :, r * tq:(r + 2) * tq]
        ot = acc[:HEAD_DIM] / acc[HEAD_DIM:HEAD_DIM + 1]
        pair = jnp.concatenate([ot[:, :tq], ot[:, tq:]], axis=0)
        o_ref[0, :, h * HEAD_DIM:(h + 2) * HEAD_DIM] = pair.T.astype(BF16)


def _attention(qt, k, vt, tq, score_bound):
    b, _, l = qt.shape
    nc, tk = vt.shape[1], vt.shape[4]

    def call(bounded):
        return pl.pallas_call(
            functools.partial(_attn_kernel, bounded=bounded),
            grid=(b, l // tq),
            in_specs=[pl.BlockSpec((1, ATTN_W, tq), lambda i, j: (i, 0, j)),
                      pl.BlockSpec((1, l, KV_W), lambda i, j: (i, 0, 0)),
                      pl.BlockSpec((1, nc, N_KV_HEADS, PV_ROWS, tk), lambda i, j: (i, 0, 0, 0, 0))],
            out_specs=pl.BlockSpec((1, tq, ATTN_W), lambda i, j: (i, j, 0)),
            out_shape=jax.ShapeDtypeStruct((b, l, ATTN_W), BF16),
            scratch_shapes=[pltpu.VMEM((N_KV_HEADS, KV_W, KV_GROUP * tq), BF16),
                            pltpu.VMEM((N_KV_HEADS, PV_ROWS, KV_GROUP * tq), F32),
                            pltpu.VMEM((N_KV_HEADS, 1, KV_GROUP * tq), F32)],
            compiler_params=_cparams("parallel", "parallel"),
            name="attention" if bounded else "attention_running_max",
        )

    return lax.cond(score_bound <= SAFE_SCORE, call(True), call(False), qt, k, vt)


def _complex_powers(lr, li, n):
    pr, pi = jnp.ones_like(lr)[None], jnp.zeros_like(li)[None]
    cr, ci = lr, li
    while pr.shape[0] < n:
        pr, pi = (jnp.concatenate([pr, pr * cr - pi * ci]), jnp.concatenate([pi, pr * ci + pi * cr]))
        cr, ci = cr * cr - ci * ci, 2.0 * cr * ci
    return pr[:n], pi[:n]


def _toeplitz_kernel(k_ref, o_ref):
    n, hch, t = k_ref.shape[1], k_ref.shape[2], k_ref.shape[3] // 2
    for i in range(n):
        for h in range(hch):
            rows = jnp.broadcast_to(k_ref[0, i, h:h + 1, :], (t, 2 * t))
            rolled = pltpu.roll(rows, t + 1, 1, stride=1, stride_axis=0)
            o_ref[0, i * t:(i + 1) * t, h * t:(h + 1) * t] = rolled[:, :t].astype(BF16)


def _toeplitz(k_cat):
    g, hin, hout, t2 = k_cat.shape
    t = t2 // 2
    n = _pick(hin, 4)
    return pl.pallas_call(
        _toeplitz_kernel,
        grid=(g, hin // n),
        in_specs=[pl.BlockSpec((1, n, hout, t2), lambda i, j: (i, j, 0, 0))],
        out_specs=pl.BlockSpec((1, n * t, hout * t), lambda i, j: (i, j, 0)),
        out_shape=jax.ShapeDtypeStruct((g, hin * t, hout * t), BF16),
        compiler_params=_cparams("parallel", "parallel"),
        name="s5_toeplitz",
    )(k_cat)


def _s5_operators(a_re, a_im, log_dt, b_re, b_im, c_re, c_im, t):
    hi = lax.Precision.HIGHEST
    a_re, a_im, log_dt = a_re.astype(F32), a_im.astype(F32), log_dt.astype(F32)
    b_re, b_im, c_re, c_im = (z.astype(F32) for z in (b_re, b_im, c_re, c_im))
    dt = jnp.exp(log_dt)[..., None]
    mag = jnp.exp(a_re * dt)
    lam_re, lam_im = mag * jnp.cos(a_im * dt), mag * jnp.sin(a_im * dt)
    num_re = lam_re - 1.0
    den = a_re * a_re + a_im * a_im
    coef_re = (num_re * a_re + lam_im * a_im) / den
    coef_im = (lam_im * a_re - num_re * a_im) / den
    pw_re, pw_im = _complex_powers(lam_re, lam_im, t + 1)
    e_re = pw_re * coef_re - pw_im * coef_im
    e_im = pw_re * coef_im + pw_im * coef_re
    g, p, hch = b_re.shape

    bt = lambda z: z.transpose(0, 2, 1)[:, :, None, :]
    cb_re = (c_re[:, None] * bt(b_re) - c_im[:, None] * bt(b_im)).reshape(g, hch * hch, p)
    cb_im = (c_re[:, None] * bt(b_im) + c_im[:, None] * bt(b_re)).reshape(g, hch * hch, p)
    kern = (jnp.einsum('gkp,tdgp->dgkt', cb_re, e_re[:t], precision=hi)
            - jnp.einsum('gkp,tdgp->dgkt', cb_im, e_im[:t], precision=hi))
    k_f, k_b = kern[0], kern[1]
    k_cat = jnp.concatenate([k_b[..., :0:-1], k_f[..., :1] + k_b[..., :1], k_f[..., 1:],
                             jnp.zeros_like(k_f[..., :1])], axis=-1)
    toep = _toeplitz(k_cat.reshape(g, hch, hch, 2 * t))

    bt_re, bt_im = b_re.transpose(0, 2, 1)[:, :, None, :], b_im.transpose(0, 2, 1)[:, :, None, :]

    def times_b(er, ei):
        er, ei = er.transpose(1, 0, 2)[:, None], ei.transpose(1, 0, 2)[:, None]
        re = er * bt_re - ei * bt_im
        im = er * bt_im + ei * bt_re
        f = lambda z: z.reshape(g, hch * t, p)
        return f(re), f(im)
    sf_re, sf_im = times_b(e_re[:t, 0][::-1], e_im[:t, 0][::-1])
    sb_re, sb_im = times_b(e_re[:t, 1], e_im[:t, 1])
    w_s = jnp.concatenate([sf_re, sf_im, sb_re, sb_im], axis=2)

    ct_re, ct_im = c_re.transpose(0, 2, 1)[:, :, :, None], c_im.transpose(0, 2, 1)[:, :, :, None]

    def c_times(pr, pi):
        pr, pi = pr.transpose(1, 2, 0)[:, :, None, :], pi.transpose(1, 2, 0)[:, :, None, :]
        re = ct_re * pr - ct_im * pi
        im = ct_re * pi + ct_im * pr
        f = lambda z: z.reshape(g, p, hch * t)
        return f(re), f(-im)
    of_re, of_im = c_times(pw_re[1:t + 1, 0], pw_im[1:t + 1, 0])
    ob_re, ob_im = c_times(pw_re[1:t + 1, 1][::-1], pw_im[1:t + 1, 1][::-1])
    w_o = jnp.concatenate([of_re, of_im, ob_re, ob_im], axis=1)

    lt_re, lt_im = pw_re[t], pw_im[t]
    a1 = jnp.concatenate([lt_re[0], lt_re[0], lt_re[1], lt_re[1]], axis=-1)[:, None, :]
    a2 = jnp.concatenate([-lt_im[0], lt_im[0], -lt_im[1], lt_im[1]], axis=-1)[:, None, :]
    return toep.astype(BF16), w_s.astype(BF16), w_o.astype(BF16), a1, a2


def _s5_chunk_rows(u_ref):
    return jnp.concatenate([u_ref[:, h, :] for h in range(u_ref.shape[1])], axis=1).astype(BF16)


def _s5_state_kernel(u_ref, ws_ref, s_ref):
    s_ref[0] = jnp.dot(_s5_chunk_rows(u_ref), ws_ref[0], preferred_element_type=F32)


def _s5_state(ut, w_s):
    r, ch, t = ut.shape
    g, k, n = w_s.shape
    return pl.pallas_call(
        _s5_state_kernel,
        grid=(g,),
        in_specs=[pl.BlockSpec((r, ch // g, t), lambda i: (0, i, 0)),
                  pl.BlockSpec((1, k, n), lambda i: (i, 0, 0))],
        out_specs=pl.BlockSpec((1, r, n), lambda i: (i, 0, 0)),
        out_shape=jax.ShapeDtypeStruct((g, r, n), F32),
        compiler_params=_cparams("parallel"),
        name="s5_state",
    )(ut, w_s)


def _s5_scan_kernel(s_ref, a1_ref, a2_ref, h_ref, *, nb):
    gb, rows, _ = s_ref.shape
    nchunk = rows // (2 * nb)
    half = 2 * SSM_STATE
    a1, a2 = a1_ref[...], a2_ref[...]
    a1f, a1b, a2f, a2b = a1[:, :, :half], a1[:, :, half:], a2[:, :, :half], a2[:, :, half:]

    def swap(z):
        return pltpu.roll(z.reshape(gb * nb, half), SSM_STATE, 1).reshape(gb, nb, half)

    def body(j, carry):
        hf, hb = carry
        rf = pl.ds(2 * j, nb, stride=2 * nchunk)
        rb = pl.ds(2 * (nchunk - 1 - j) + 1, nb, stride=2 * nchunk)
        h_ref[:, rf, :] = hf
        h_ref[:, rb, :] = hb
        hf = a1f * hf + a2f * swap(hf) + s_ref[:, rf, :]
        hb = a1b * hb + a2b * swap(hb) + s_ref[:, rb, :]
        return hf, hb

    zero = jnp.zeros((gb, nb, half), F32)
    lax.fori_loop(0, nchunk, body, (zero, zero))


def _s5_scan(s, a1, a2, nb, gb):
    g, r, n = s.shape
    half = n // 2
    blk = pl.BlockSpec((gb, 2 * r, half), lambda i: (i, 0, 0))
    par = pl.BlockSpec((gb, 1, n), lambda i: (i, 0, 0))
    return pl.pallas_call(
        functools.partial(_s5_scan_kernel, nb=nb),
        grid=(g // gb,),
        in_specs=[blk, par, par],
        out_specs=blk,
        out_shape=jax.ShapeDtypeStruct((g, 2 * r, half), F32),
        compiler_params=_cparams("parallel"),
        name="s5_scan",
    )(s.reshape(g, 2 * r, half), a1, a2).reshape(g, r, n)


def _s5_out_kernel(u_ref, h_ref, toep_ref, wo_ref, y_ref):
    _, hch, t = u_ref.shape
    y = jnp.dot(_s5_chunk_rows(u_ref), toep_ref[0], preferred_element_type=F32)
    y = y + jnp.dot(h_ref[0].astype(BF16), wo_ref[0], preferred_element_type=F32)
    for h in range(hch):
        y_ref[:, h, :] = y[:, h * t:(h + 1) * t]


def _s5_out(ut, hin, toep, w_o):
    r, ch, t = ut.shape
    g, n, k = w_o.shape
    blk = pl.BlockSpec((r, ch // g, t), lambda i: (0, i, 0))
    return pl.pallas_call(
        _s5_out_kernel,
        grid=(g,),
        in_specs=[blk, pl.BlockSpec((1, r, n), lambda i: (i, 0, 0)),
                  pl.BlockSpec((1, k, k), lambda i: (i, 0, 0)),
                  pl.BlockSpec((1, n, k), lambda i: (i, 0, 0))],
        out_specs=blk,
        out_shape=jax.ShapeDtypeStruct((r, ch, t), F32),
        compiler_params=_cparams("parallel"),
        name="s5_out",
    )(ut, hin, toep, w_o)


def _s5_scan_branch(ut, ops):
    toep, w_s, w_o, a1, a2 = ops
    b, c, ch, t = ut.shape
    ut = ut.reshape(b * c, ch, t)
    s = _s5_state(ut, w_s)
    hin = _s5_scan(s, a1, a2, b, 4)
    return _s5_out(ut, hin, toep, w_o).reshape(b, c, ch, t)


def _mem_kv_kernel(m_ref, g_ref, w_ref, mk_ref, mvt_ref):
    h = _rms(m_ref[0], g_ref[...]).astype(BF16)
    kv = jnp.dot(h, w_ref[...], preferred_element_type=F32)
    mk_ref[0] = kv[:, :CROSS_W].astype(BF16)
    vt = kv[:, CROSS_W:].T.astype(BF16)
    ones = jnp.ones((PV_ROWS - HEAD_DIM, vt.shape[1]), BF16)
    for hd in range(N_CROSS_HEADS):
        mvt_ref[0, hd, :HEAD_DIM, :] = vt[hd * HEAD_DIM:(hd + 1) * HEAD_DIM, :]
        mvt_ref[0, hd, HEAD_DIM:, :] = ones


def _mem_kv(mem, gain, w):
    b, m, d = mem.shape
    n = w.shape[1]
    return pl.pallas_call(
        _mem_kv_kernel,
        grid=(b,),
        in_specs=[pl.BlockSpec((1, m, d), lambda i: (i, 0, 0)), _const_spec((1, d)), _const_spec((d, n))],
        out_specs=(pl.BlockSpec((1, m, CROSS_W), lambda i: (i, 0, 0)),
                   pl.BlockSpec((1, N_CROSS_HEADS, PV_ROWS, m), lambda i: (i, 0, 0, 0))),
        out_shape=(jax.ShapeDtypeStruct((b, m, CROSS_W), BF16),
                   jax.ShapeDtypeStruct((b, N_CROSS_HEADS, PV_ROWS, m), BF16)),
        compiler_params=_cparams("parallel"),
        name="mem_kv",
    )(mem, gain, w)


def _merge_kernel(x_ref, a_ref, y_ref, u_ref, qct_ref, gate_ref, mk_ref, mvt_ref, dskip_ref, wglu_ref,
                  pa_ref, ps_ref, pc_ref, wout_ref, o_ref):
    d = x_ref.shape[2]
    heads = []
    for h in range(N_CROSS_HEADS):
        mk = _head_lanes(mk_ref[0, :, (h // 2) * LANES:(h // 2 + 1) * LANES], h)
        st = jnp.dot(mk, qct_ref[0, h * HEAD_DIM:(h + 1) * HEAD_DIM, :], preferred_element_type=F32)
        pt = jnp.exp(st - jnp.max(st, axis=0, keepdims=True)).astype(BF16)
        ot = jnp.dot(mvt_ref[0, h], pt, preferred_element_type=F32)
        heads.append(ot[:HEAD_DIM] / ot[HEAD_DIM:HEAD_DIM + 1])
    cross = jnp.concatenate(heads, axis=0).T.astype(BF16)
    y = jnp.concatenate([y_ref[0, c].T for c in range(y_ref.shape[1])], axis=0)
    y = y + dskip_ref[...] * u_ref[0].astype(F32)
    y = jax.nn.gelu(y)
    y = y * _sigmoid(jnp.dot(y.astype(BF16), wglu_ref[...], preferred_element_type=F32))
    yb = y.astype(BF16)
    out = x_ref[0]
    half = d // 2
    for c0 in range(0, d, half):
        cols = slice(c0, c0 + half)
        gate = lambda j: gate_ref[0, :, j * d + c0:j * d + c0 + half].astype(F32)
        m = gate(0) * jnp.dot(a_ref[0], pa_ref[:, cols], preferred_element_type=F32)
        m = m + gate(1) * jnp.dot(yb, ps_ref[:, cols], preferred_element_type=F32)
        m = m + gate(2) * jnp.dot(cross, pc_ref[:, cols], preferred_element_type=F32)
        out = out + jnp.dot(m.astype(BF16), wout_ref[cols, :], preferred_element_type=F32)
    o_ref[0] = out


def _merge(x, attn, yscan, u, qct, gates, mk, mvt, d_skip, w_glu, p_attn, p_ssm, p_cross, w_out, tl):
    b, l, d = x.shape
    row = lambda width: pl.BlockSpec((1, tl, width), lambda i, j: (i, j, 0))
    col = lambda height: pl.BlockSpec((1, height, tl), lambda i, j: (i, 0, j))
    n_mem = mk.shape[1]
    return pl.pallas_call(
        _merge_kernel,
        grid=(b, l // tl),
        in_specs=[row(d), row(ATTN_W),
                  pl.BlockSpec((1, tl // S5_CHUNK, SSM_W, S5_CHUNK), lambda i, j: (i, j, 0, 0)),
                  row(SSM_W), col(CROSS_W), row(N_BRANCHES * d),
                  pl.BlockSpec((1, n_mem, CROSS_W), lambda i, j: (i, 0, 0)),
                  pl.BlockSpec((1, N_CROSS_HEADS, PV_ROWS, n_mem), lambda i, j: (i, 0, 0, 0)),
                  _const_spec((1, SSM_W)), _const_spec((SSM_W, SSM_W)),
                  _const_spec((ATTN_W, d)), _const_spec((SSM_W, d)), _const_spec((CROSS_W, d)),
                  _const_spec((d, d))],
        out_specs=row(d),
        out_shape=jax.ShapeDtypeStruct((b, l, d), F32),
        compiler_params=_cparams("parallel", "parallel"),
        name="merge",
    )(x, attn, yscan, u, qct, gates, mk, mvt, d_skip, w_glu, p_attn, p_ssm, p_cross, w_out)


MXU_TILE = 256


def _ffn_chunks(dff, tiles_per_chunk=11):
    step = tiles_per_chunk * MXU_TILE
    return [min(step, dff - c0) for c0 in range(0, dff, step)]


def _ffn_kernel(xp_ref, x_ref, xn_ref, g_ref, wup_ref, cw_ref, cb_ref, wdn_ref, gf_ref, o_ref,
                h_scr, gate_scr, *, final_norm):
    j = pl.program_id(1)
    nj = pl.num_programs(1)
    tm = x_ref.shape[1]
    dff = wdn_ref.shape[0]
    gain = g_ref[...]
    x = x_ref[0]
    h_scr[0:HALO, :] = jnp.where(j == 0, 0.0, _rms(xp_ref[0], gain)).astype(BF16)
    h_scr[HALO:HALO + tm, :] = _rms(x, gain).astype(BF16)
    h_scr[HALO + tm:, :] = jnp.where(j == nj - 1, 0.0, _rms(xn_ref[0], gain)).astype(BF16)
    y = x
    c0 = 0
    for wc in _ffn_chunks(dff):
        gs = gate_scr.at[:, c0:c0 + wc]
        gs[...] = jnp.dot(h_scr[...], wup_ref[:, c0:c0 + wc], preferred_element_type=F32)
        val = jnp.dot(h_scr[HALO:HALO + tm, :], wup_ref[:, dff + c0:dff + c0 + wc], preferred_element_type=F32)
        cw = cw_ref[:, c0:c0 + wc]
        gc = (gs[HALO - 1:HALO - 1 + tm, :] * cw[0:1, :] + gs[HALO:HALO + tm, :] * cw[1:2, :]
              + gs[HALO + 1:HALO + 1 + tm, :] * cw[2:3, :] + cb_ref[:, c0:c0 + wc])
        act = (jax.nn.gelu(gc) * val).astype(BF16)
        y = y + jnp.dot(act, wdn_ref[c0:c0 + wc, :], preferred_element_type=F32)
        c0 += wc
    if final_norm:
        y = _rms(y, gf_ref[...])
    o_ref[0] = y


def _ffn(x, gain, w_up, conv_w, conv_b, w_down, final_gain, final_norm, tm):
    b, l, d = x.shape
    dff = w_down.shape[0]
    r = tm // HALO
    nh = l // HALO
    return pl.pallas_call(
        functools.partial(_ffn_kernel, final_norm=final_norm),
        grid=(b, l // tm),
        in_specs=[pl.BlockSpec((1, HALO, d), lambda i, j: (i, jnp.maximum(j * r - 1, 0), 0)),
                  pl.BlockSpec((1, tm, d), lambda i, j: (i, j, 0)),
                  pl.BlockSpec((1, HALO, d), lambda i, j: (i, jnp.minimum((j + 1) * r, nh - 1), 0)),
                  _const_spec((1, d)), _const_spec((d, 2 * dff)), _const_spec((3, dff)),
                  _const_spec((1, dff)), _const_spec((dff, d)), _const_spec((1, d))],
        out_specs=pl.BlockSpec((1, tm, d), lambda i, j: (i, j, 0)),
        out_shape=jax.ShapeDtypeStruct((b, l, d), F32),
        scratch_shapes=[pltpu.VMEM((tm + 2 * HALO, d), BF16),
                        pltpu.VMEM((tm + 2 * HALO, dff), F32)],
        compiler_params=_cparams("parallel", "arbitrary"),
        name="ffn",
    )(x, x, x, gain, w_up, conv_w, conv_b, w_down, final_gain)


def _rope_tables(l):
    rows = l // GRID_W
    r = jnp.broadcast_to(jnp.arange(rows, dtype=F32)[:, None], (rows, GRID_W)).reshape(l)
    c = jnp.broadcast_to(jnp.arange(GRID_W, dtype=F32)[None, :], (rows, GRID_W)).reshape(l)
    freqs = ROPE_THETA ** (-jnp.arange(ROPE_PAIRS, dtype=F32) / ROPE_PAIRS)
    ang_r, ang_c = r[:, None] * freqs, c[:, None] * freqs
    ang = jnp.concatenate([ang_r, ang_r, ang_c, ang_c], axis=-1)
    cos, sin = jnp.cos(ang), jnp.sin(ang)
    first = (jnp.arange(HEAD_DIM) % (2 * ROPE_PAIRS)) < ROPE_PAIRS
    sin_up = jnp.where(first, -sin, 0.0)
    sin_dn = jnp.where(first, 0.0, sin)
    two = lambda a: jnp.concatenate([a, a], axis=1)
    return two(cos), two(sin_up), two(sin_dn)


def _prepare_layer(p, l):
    bf = lambda a: a.astype(BF16)
    row = lambda a: a.astype(F32).reshape(1, -1)
    max_abs = lambda a: jnp.max(jnp.abs(a.astype(F32)))
    score_bound = 1.02 * QK_SCALE * HEAD_DIM * max_abs(p['q_norm'][l]) * max_abs(p['k_norm'][l])
    return dict(
        norm_mix=row(p['norm_mix'][l]), w_in=bf(p['w_in'][l]),
        q_gain=row(jnp.tile(p['q_norm'][l], N_HEADS)), k_gain=row(jnp.tile(p['k_norm'][l], N_KV_HEADS)),
        score_bound=score_bound,
        s5=_s5_operators(p['ssm_a_re'][l], p['ssm_a_im'][l], p['ssm_log_dt'][l], p['ssm_b_re'][l],
                         p['ssm_b_im'][l], p['ssm_c_re'][l], p['ssm_c_im'][l], S5_CHUNK),
        d_skip=row(p['ssm_d'][l]), w_glu=bf(p['ssm_glu'][l]),
        mem_norm=row(p['mem_norm'][l]), w_mem_kv=bf(p['w_mem_kv'][l]),
        p_attn=bf(p['p_attn'][l]), p_ssm=bf(p['p_ssm'][l]), p_cross=bf(p['p_cross'][l]), w_out=bf(p['w_out'][l]),
        norm_ffn=row(p['norm_ffn'][l]), w_up=bf(p['w_up'][l]), conv_w=p['conv_w'][l].astype(F32),
        conv_b=row(p['conv_b'][l]), w_down=bf(p['w_down'][l]),
    )


def _encode(x, mem, layers, final_gain):
    b, l, d = x.shape
    tables = _rope_tables(l)
    tk = _pick(l, 512)
    tq = _pick(l, 1024)
    tm = _pick(l, 512)
    for li, w in enumerate(layers):
        qt, k, vt, u, ut, qct, gates = _in_proj(x, w['norm_mix'], w['w_in'], tables, w['q_gain'], w['k_gain'], tk)
        attn = _attention(qt, k, vt, tq, w['score_bound'])
        yscan = _s5_scan_branch(ut, w['s5'])
        mk, mvt = _mem_kv(mem, w['mem_norm'], w['w_mem_kv'])
        x = _merge(x, attn, yscan, u, qct, gates, mk, mvt, w['d_skip'], w['w_glu'], w['p_attn'], w['p_ssm'],
                   w['p_cross'], w['w_out'], tm)
        x = _ffn(x, w['norm_ffn'], w['w_up'], w['conv_w'], w['conv_b'], w['w_down'], final_gain,
                 li == len(layers) - 1, tm)
    return x


def kernel(x_prompt, x_sample, mem_prompt, mem_sample, norm_mix, w_in, q_norm, k_norm, ssm_a_re, ssm_a_im, ssm_log_dt, ssm_b_re, ssm_b_im, ssm_c_re, ssm_c_im, ssm_d, ssm_glu, mem_norm, w_mem_kv, p_attn, p_ssm, p_cross, w_out, norm_ffn, w_up, conv_w, conv_b, w_down, norm_final):
    p = dict(norm_mix=norm_mix, w_in=w_in, q_norm=q_norm, k_norm=k_norm, ssm_a_re=ssm_a_re, ssm_a_im=ssm_a_im,
             ssm_log_dt=ssm_log_dt, ssm_b_re=ssm_b_re, ssm_b_im=ssm_b_im, ssm_c_re=ssm_c_re, ssm_c_im=ssm_c_im,
             ssm_d=ssm_d, ssm_glu=ssm_glu, mem_norm=mem_norm, w_mem_kv=w_mem_kv, p_attn=p_attn, p_ssm=p_ssm,
             p_cross=p_cross, w_out=w_out, norm_ffn=norm_ffn, w_up=w_up, conv_w=conv_w, conv_b=conv_b,
             w_down=w_down)
    layers = [_prepare_layer(p, l) for l in range(norm_mix.shape[0])]
    final_gain = norm_final.astype(F32).reshape(1, -1)
    y_prompt = _encode(x_prompt, mem_prompt, layers, final_gain)
    y_sample = _encode(x_sample, mem_sample, layers, final_gain)
    return (y_prompt, y_sample)
```

```python
import functools

import jax
import jax.numpy as jnp
from jax import lax
from jax.experimental import pallas as pl
from jax.experimental.pallas import tpu as pltpu

HEAD_DIM = 64
N_HEADS = 8
N_KV_HEADS = 2
KV_GROUP = N_HEADS // N_KV_HEADS
ATTN_W = N_HEADS * HEAD_DIM
KV_W = N_KV_HEADS * HEAD_DIM
SSM_GROUP_CH = 16
SSM_W = 256
SSM_GROUPS = SSM_W // SSM_GROUP_CH
SSM_STATE = 64
N_CROSS_HEADS = 4
CROSS_W = N_CROSS_HEADS * HEAD_DIM
N_BRANCHES = 3
GRID_W = 64
ROPE_THETA = 10000.0
ROPE_PAIRS = HEAD_DIM // 4
EPS = 1e-6

LOG2E = 1.4426950408889634
QK_SCALE = HEAD_DIM ** -0.5 * LOG2E
SAFE_SCORE = 64.0
LANES = 128
S5_CHUNK = LANES
PV_ROWS = 2 * HEAD_DIM
HALO = 16
ROW_TILE = 512
QUERY_TILE = 1024
VMEM_LIMIT = 56 * 1024 * 1024

F32 = jnp.float32
BF16 = jnp.bfloat16


def _cparams(*sem):
    return pltpu.CompilerParams(dimension_semantics=sem, vmem_limit_bytes=VMEM_LIMIT)


def _const_spec(shape):
    nd = len(shape)
    return pl.BlockSpec(shape, lambda *_: (0,) * nd, pipeline_mode=pl.Buffered(1))


def _rms(x, gain):
    return x * lax.rsqrt(jnp.mean(x * x, axis=-1, keepdims=True) + EPS) * gain


def _sigmoid(x):
    return 0.5 * jnp.tanh(0.5 * x) + 0.5


def _pick(n, pref):
    t = min(n, pref)
    while n % t:
        t //= 2
    return t


def _head_norm_rope(x, gain, cos, sin_up, sin_dn, out_scale):
    w = x.shape[1]
    reps = w // LANES
    tile = lambda a: a if reps == 1 else jnp.concatenate([a] * reps, axis=1)
    xg = x * gain
    y = (xg * tile(cos)
         + pltpu.roll(xg, w - ROPE_PAIRS, 1) * tile(sin_up)
         + pltpu.roll(xg, ROPE_PAIRS, 1) * tile(sin_dn))
    lane = lax.broadcasted_iota(jnp.int32, (1, LANES), 1)
    low = lane < HEAD_DIM
    outs = []
    for p in range(reps):
        xp = x[:, p * LANES:(p + 1) * LANES]
        sq = xp * xp
        ss_all = jnp.sum(sq, axis=-1, keepdims=True)
        ss_lo = jnp.sum(jnp.where(low, sq, 0.0), axis=-1, keepdims=True)
        r_lo = lax.rsqrt(ss_lo * (1.0 / HEAD_DIM) + EPS) * out_scale
        r_hi = lax.rsqrt((ss_all - ss_lo) * (1.0 / HEAD_DIM) + EPS) * out_scale
        outs.append(y[:, p * LANES:(p + 1) * LANES] * jnp.where(low, r_lo, r_hi))
    return outs[0] if reps == 1 else jnp.concatenate(outs, axis=1)


def _in_proj_kernel(x_ref, g_ref, w_ref, cos_ref, sup_ref, sdn_ref, qg_ref, kg_ref,
                    qt_ref, k_ref, vt_ref, u_ref, ut_ref, qc_ref, gate_ref, h_scr):
    h_scr[...] = _rms(x_ref[0], g_ref[...]).astype(BF16)
    z = jnp.dot(h_scr[...], w_ref[...], preferred_element_type=F32)

    def seg(a, b):
        return z[:, a:b]

    cos, sup, sdn = cos_ref[...], sup_ref[...], sdn_ref[...]
    o = 0
    q = seg(o, o + ATTN_W)
    qt_ref[0] = _head_norm_rope(q, qg_ref[...], cos, sup, sdn, QK_SCALE).T.astype(BF16)
    o += ATTN_W
    kv = seg(o, o + 2 * KV_W)
    k_ref[0] = _head_norm_rope(kv[:, :KV_W], kg_ref[...], cos, sup, sdn, 1.0).astype(BF16)
    vt = kv[:, KV_W:].T.astype(BF16)
    ones = jnp.ones((PV_ROWS - HEAD_DIM, vt.shape[1]), BF16)
    for g in range(N_KV_HEADS):
        vt_ref[0, 0, g, :HEAD_DIM, :] = vt[g * HEAD_DIM:(g + 1) * HEAD_DIM, :]
        vt_ref[0, 0, g, HEAD_DIM:, :] = ones
    o += 2 * KV_W
    u = seg(o, o + SSM_W)
    u_ref[0] = u.astype(BF16)
    ut = u.T
    for c in range(ut_ref.shape[1]):
        ut_ref[0, c] = ut[:, c * S5_CHUNK:(c + 1) * S5_CHUNK]
    o += SSM_W
    qc_ref[0] = (seg(o, o + CROSS_W) * HEAD_DIM ** -0.5).T.astype(BF16)
    o += CROSS_W
    d = x_ref.shape[2]
    for j in range(N_BRANCHES):
        gate_ref[0, :, j * d:(j + 1) * d] = _sigmoid(seg(o + j * d, o + (j + 1) * d)).astype(BF16)


def _in_proj(x, gain, w, tables, q_gain, k_gain, tk):
    b, l, d = x.shape
    n_in = w.shape[1]
    nj = l // tk
    row = lambda width: pl.BlockSpec((1, tk, width), lambda i, j: (i, j, 0))
    tab = pl.BlockSpec((tk, LANES), lambda i, j: (j, 0))
    out_shape = (
        jax.ShapeDtypeStruct((b, ATTN_W, l), BF16),
        jax.ShapeDtypeStruct((b, l, KV_W), BF16),
        jax.ShapeDtypeStruct((b, nj, N_KV_HEADS, PV_ROWS, tk), BF16),
        jax.ShapeDtypeStruct((b, l, SSM_W), BF16),
        jax.ShapeDtypeStruct((b, l // S5_CHUNK, SSM_W, S5_CHUNK), F32),
        jax.ShapeDtypeStruct((b, CROSS_W, l), BF16),
        jax.ShapeDtypeStruct((b, l, N_BRANCHES * d), BF16),
    )
    col = lambda height: pl.BlockSpec((1, height, tk), lambda i, j: (i, 0, j))
    return pl.pallas_call(
        _in_proj_kernel,
        grid=(b, nj),
        in_specs=[row(d), _const_spec((1, d)), _const_spec((d, n_in)), tab, tab, tab,
                  _const_spec((1, ATTN_W)), _const_spec((1, KV_W))],
        out_specs=(col(ATTN_W), row(KV_W),
                   pl.BlockSpec((1, 1, N_KV_HEADS, PV_ROWS, tk), lambda i, j: (i, j, 0, 0, 0)),
                   row(SSM_W), pl.BlockSpec((1, tk // S5_CHUNK, SSM_W, S5_CHUNK), lambda i, j: (i, j, 0, 0)),
                   col(CROSS_W), row(N_BRANCHES * d)),
        out_shape=out_shape,
        scratch_shapes=[pltpu.VMEM((tk, d), BF16)],
        compiler_params=_cparams("parallel", "parallel"),
        name="in_proj",
    )(x, gain, w, *tables, q_gain, k_gain)


def _head_lanes(pair, h):
    return pair[:, (h % 2) * HEAD_DIM:(h % 2 + 1) * HEAD_DIM]


def _attn_kernel(qt_ref, k_ref, vt_ref, o_ref, qs_ref, acc_ref, m_ref, *, bounded):
    tq = qt_ref.shape[2]
    nc, tk = vt_ref.shape[1], vt_ref.shape[4]
    qs_ref[...] = jnp.zeros(qs_ref.shape, BF16)
    for h in range(N_HEADS):
        g, r = divmod(h, KV_GROUP)
        qs_ref[g, g * HEAD_DIM:(g + 1) * HEAD_DIM, r * tq:(r + 1) * tq] = qt_ref[0, h * HEAD_DIM:(h + 1) * HEAD_DIM, :]
    acc_ref[...] = jnp.zeros(acc_ref.shape, F32)
    if not bounded:
        m_ref[...] = jnp.full(m_ref.shape, -jnp.inf, F32)

    def body(c, carry):
        kc = k_ref[0, pl.ds(pl.multiple_of(c * tk, tk), tk), :]
        for g in range(N_KV_HEADS):
            st = jnp.dot(kc, qs_ref[g], preferred_element_type=F32)
            if bounded:
                pt = jnp.exp2(st).astype(BF16)
                acc_ref[g] += jnp.dot(vt_ref[0, c, g], pt, preferred_element_type=F32)
            else:
                m_old = m_ref[g]
                m_new = jnp.maximum(m_old, jnp.max(st, axis=0, keepdims=True))
                pt = jnp.exp2(st - m_new).astype(BF16)
                acc_ref[g] = (jnp.exp2(m_old - m_new) * acc_ref[g]
                              + jnp.dot(vt_ref[0, c, g], pt, preferred_element_type=F32))
                m_ref[g] = m_new
        return carry

    lax.fori_loop(0, nc, body, 0, unroll=4)
    for h in range(0, N_HEADS, 2):
        g, r = divmod(h, KV_GROUP)
        acc = acc_ref[g, :, r * tq:(r + 2) * tq]
        ot = acc[:HEAD_DIM] / acc[HEAD_DIM:HEAD_DIM + 1]
        pair = jnp.concatenate([ot[:, :tq], ot[:, tq:]], axis=0)
        o_ref[0, :, h * HEAD_DIM:(h + 2) * HEAD_DIM] = pair.T.astype(BF16)


def _attention(qt, k, vt, tq, score_bound):
    b, _, l = qt.shape
    nc, tk = vt.shape[1], vt.shape[4]

    def call(bounded):
        return pl.pallas_call(
            functools.partial(_attn_kernel, bounded=bounded),
            grid=(b, l // tq),
            in_specs=[pl.BlockSpec((1, ATTN_W, tq), lambda i, j: (i, 0, j)),
                      pl.BlockSpec((1, l, KV_W), lambda i, j: (i, 0, 0)),
                      pl.BlockSpec((1, nc, N_KV_HEADS, PV_ROWS, tk), lambda i, j: (i, 0, 0, 0, 0))],
            out_specs=pl.BlockSpec((1, tq, ATTN_W), lambda i, j: (i, j, 0)),
            out_shape=jax.ShapeDtypeStruct((b, l, ATTN_W), BF16),
            scratch_shapes=[pltpu.VMEM((N_KV_HEADS, KV_W, KV_GROUP * tq), BF16),
                            pltpu.VMEM((N_KV_HEADS, PV_ROWS, KV_GROUP * tq), F32),
                            pltpu.VMEM((N_KV_HEADS, 1, KV_GROUP * tq), F32)],
            compiler_params=_cparams("parallel", "parallel"),
            name="attention" if bounded else "attention_running_max",
        )

    return lax.cond(score_bound <= SAFE_SCORE, call(True), call(False), qt, k, vt)


def _complex_powers(lr, li, n):
    pr, pi = jnp.ones_like(lr)[None], jnp.zeros_like(li)[None]
    cr, ci = lr, li
    while pr.shape[0] < n:
        pr, pi = (jnp.concatenate([pr, pr * cr - pi * ci]), jnp.concatenate([pi, pr * ci + pi * cr]))
        cr, ci = cr * cr - ci * ci, 2.0 * cr * ci
    return pr[:n], pi[:n]


def _toeplitz_kernel(k_ref, o_ref):
    n, hch, t = k_ref.shape[1], k_ref.shape[2], k_ref.shape[3] // 2
    for i in range(n):
        for h in range(hch):
            rows = jnp.broadcast_to(k_ref[0, i, h:h + 1, :], (t, 2 * t))
            rolled = pltpu.roll(rows, t + 1, 1, stride=1, stride_axis=0)
            o_ref[0, i * t:(i + 1) * t, h * t:(h + 1) * t] = rolled[:, :t].astype(BF16)


def _toeplitz(k_cat):
    g, hin, hout, t2 = k_cat.shape
    t = t2 // 2
    n = _pick(hin, 4)
    return pl.pallas_call(
        _toeplitz_kernel,
        grid=(g, hin // n),
        in_specs=[pl.BlockSpec((1, n, hout, t2), lambda i, j: (i, j, 0, 0))],
        out_specs=pl.BlockSpec((1, n * t, hout * t), lambda i, j: (i, j, 0)),
        out_shape=jax.ShapeDtypeStruct((g, hin * t, hout * t), BF16),
        compiler_params=_cparams("parallel", "parallel"),
        name="s5_toeplitz",
    )(k_cat)


def _s5_operators(a_re, a_im, log_dt, b_re, b_im, c_re, c_im, t):
    hi = lax.Precision.HIGHEST
    a_re, a_im, log_dt = a_re.astype(F32), a_im.astype(F32), log_dt.astype(F32)
    b_re, b_im, c_re, c_im = (z.astype(F32) for z in (b_re, b_im, c_re, c_im))
    dt = jnp.exp(log_dt)[..., None]
    mag = jnp.exp(a_re * dt)
    lam_re, lam_im = mag * jnp.cos(a_im * dt), mag * jnp.sin(a_im * dt)
    num_re = lam_re - 1.0
    den = a_re * a_re + a_im * a_im
    coef_re = (num_re * a_re + lam_im * a_im) / den
    coef_im = (lam_im * a_re - num_re * a_im) / den
    pw_re, pw_im = _complex_powers(lam_re, lam_im, t + 1)
    e_re = pw_re * coef_re - pw_im * coef_im
    e_im = pw_re * coef_im + pw_im * coef_re
    g, p, hch = b_re.shape

    bt = lambda z: z.transpose(0, 2, 1)[:, :, None, :]
    cb_re = (c_re[:, None] * bt(b_re) - c_im[:, None] * bt(b_im)).reshape(g, hch * hch, p)
    cb_im = (c_re[:, None] * bt(b_im) + c_im[:, None] * bt(b_re)).reshape(g, hch * hch, p)
    kern = (jnp.einsum('gkp,tdgp->dgkt', cb_re, e_re[:t], precision=hi)
            - jnp.einsum('gkp,tdgp->dgkt', cb_im, e_im[:t], precision=hi))
    k_f, k_b = kern[0], kern[1]
    k_cat = jnp.concatenate([k_b[..., :0:-1], k_f[..., :1] + k_b[..., :1], k_f[..., 1:],
                             jnp.zeros_like(k_f[..., :1])], axis=-1)
    toep = _toeplitz(k_cat.reshape(g, hch, hch, 2 * t))

    bt_re, bt_im = b_re.transpose(0, 2, 1)[:, :, None, :], b_im.transpose(0, 2, 1)[:, :, None, :]

    def times_b(er, ei):
        er, ei = er.transpose(1, 0, 2)[:, None], ei.transpose(1, 0, 2)[:, None]
        re = er * bt_re - ei * bt_im
        im = er * bt_im + ei * bt_re
        f = lambda z: z.reshape(g, hch * t, p)
        return f(re), f(im)
    sf_re, sf_im = times_b(e_re[:t, 0][::-1], e_im[:t, 0][::-1])
    sb_re, sb_im = times_b(e_re[:t, 1], e_im[:t, 1])
    w_s = jnp.concatenate([sf_re, sf_im, sb_re, sb_im], axis=2)

    ct_re, ct_im = c_re.transpose(0, 2, 1)[:, :, :, None], c_im.transpose(0, 2, 1)[:, :, :, None]

    def c_times(pr, pi):
        pr, pi = pr.transpose(1, 2, 0)[:, :, None, :], pi.transpose(1, 2, 0)[:, :, None, :]
        re = ct_re * pr - ct_im * pi
        im = ct_re * pi + ct_im * pr
        f = lambda z: z.reshape(g, p, hch * t)
        return f(re), f(-im)
    of_re, of_im = c_times(pw_re[1:t + 1, 0], pw_im[1:t + 1, 0])
    ob_re, ob_im = c_times(pw_re[1:t + 1, 1][::-1], pw_im[1:t + 1, 1][::-1])
    w_o = jnp.concatenate([of_re, of_im, ob_re, ob_im], axis=1)

    lt_re, lt_im = pw_re[t], pw_im[t]
    a1 = jnp.concatenate([lt_re[0], lt_re[0], lt_re[1], lt_re[1]], axis=-1)[:, None, :]
    a2 = jnp.concatenate([-lt_im[0], lt_im[0], -lt_im[1], lt_im[1]], axis=-1)[:, None, :]
    return toep.astype(BF16), w_s.astype(BF16), w_o.astype(BF16), a1, a2


def _s5_chunk_rows(u_ref):
    return jnp.concatenate([u_ref[:, h, :] for h in range(u_ref.shape[1])], axis=1).astype(BF16)


def _s5_state_kernel(u_ref, ws_ref, s_ref):
    s_ref[0] = jnp.dot(_s5_chunk_rows(u_ref), ws_ref[0], preferred_element_type=F32)


def _s5_state(ut, w_s):
    r, ch, t = ut.shape
    g, k, n = w_s.shape
    return pl.pallas_call(
        _s5_state_kernel,
        grid=(g,),
        in_specs=[pl.BlockSpec((r, ch // g, t), lambda i: (0, i, 0)),
                  pl.BlockSpec((1, k, n), lambda i: (i, 0, 0))],
        out_specs=pl.BlockSpec((1, r, n), lambda i: (i, 0, 0)),
        out_shape=jax.ShapeDtypeStruct((g, r, n), F32),
        compiler_params=_cparams("parallel"),
        name="s5_state",
    )(ut, w_s)


def _s5_scan_kernel(s_ref, a1_ref, a2_ref, h_ref, *, nb):
    gb, rows, _ = s_ref.shape
    nchunk = rows // (2 * nb)
    half = 2 * SSM_STATE
    a1, a2 = a1_ref[...], a2_ref[...]
    a1f, a1b, a2f, a2b = a1[:, :, :half], a1[:, :, half:], a2[:, :, :half], a2[:, :, half:]

    def swap(z):
        return pltpu.roll(z.reshape(gb * nb, half), SSM_STATE, 1).reshape(gb, nb, half)

    def body(j, carry):
        hf, hb = carry
        rf = pl.ds(2 * j, nb, stride=2 * nchunk)
        rb = pl.ds(2 * (nchunk - 1 - j) + 1, nb, stride=2 * nchunk)
        h_ref[:, rf, :] = hf
        h_ref[:, rb, :] = hb
        hf = a1f * hf + a2f * swap(hf) + s_ref[:, rf, :]
        hb = a1b * hb + a2b * swap(hb) + s_ref[:, rb, :]
        return hf, hb

    zero = jnp.zeros((gb, nb, half), F32)
    lax.fori_loop(0, nchunk, body, (zero, zero))


def _s5_scan(s, a1, a2, nb, gb):
    g, r, n = s.shape
    half = n // 2
    blk = pl.BlockSpec((gb, 2 * r, half), lambda i: (i, 0, 0))
    par = pl.BlockSpec((gb, 1, n), lambda i: (i, 0, 0))
    return pl.pallas_call(
        functools.partial(_s5_scan_kernel, nb=nb),
        grid=(g // gb,),
        in_specs=[blk, par, par],
        out_specs=blk,
        out_shape=jax.ShapeDtypeStruct((g, 2 * r, half), F32),
        compiler_params=_cparams("parallel"),
        name="s5_scan",
    )(s.reshape(g, 2 * r, half), a1, a2).reshape(g, r, n)


def _s5_out_kernel(u_ref, h_ref, toep_ref, wo_ref, y_ref):
    _, hch, t = u_ref.shape
    y = jnp.dot(_s5_chunk_rows(u_ref), toep_ref[0], preferred_element_type=F32)
    y = y + jnp.dot(h_ref[0].astype(BF16), wo_ref[0], preferred_element_type=F32)
    for h in range(hch):
        y_ref[:, h, :] = y[:, h * t:(h + 1) * t]


def _s5_out(ut, hin, toep, w_o):
    r, ch, t = ut.shape
    g, n, k = w_o.shape
    blk = pl.BlockSpec((r, ch // g, t), lambda i: (0, i, 0))
    return pl.pallas_call(
        _s5_out_kernel,
        grid=(g,),
        in_specs=[blk, pl.BlockSpec((1, r, n), lambda i: (i, 0, 0)),
                  pl.BlockSpec((1, k, k), lambda i: (i, 0, 0)),
                  pl.BlockSpec((1, n, k), lambda i: (i, 0, 0))],
        out_specs=blk,
        out_shape=jax.ShapeDtypeStruct((r, ch, t), F32),
        compiler_params=_cparams("parallel"),
        name="s5_out",
    )(ut, hin, toep, w_o)


def _s5_scan_branch(ut, ops):
    toep, w_s, w_o, a1, a2 = ops
    b, c, ch, t = ut.shape
    ut = ut.reshape(b * c, ch, t)
    s = _s5_state(ut, w_s)
    hin = _s5_scan(s, a1, a2, b, 4)
    return _s5_out(ut, hin, toep, w_o).reshape(b, c, ch, t)


def _mem_kv_kernel(m_ref, g_ref, w_ref, mk_ref, mvt_ref):
    h = _rms(m_ref[0], g_ref[...]).astype(BF16)
    kv = jnp.dot(h, w_ref[...], preferred_element_type=F32)
    mk_ref[0] = kv[:, :CROSS_W].astype(BF16)
    vt = kv[:, CROSS_W:].T.astype(BF16)
    ones = jnp.ones((PV_ROWS - HEAD_DIM, vt.shape[1]), BF16)
    for hd in range(N_CROSS_HEADS):
        mvt_ref[0, hd, :HEAD_DIM, :] = vt[hd * HEAD_DIM:(hd + 1) * HEAD_DIM, :]
        mvt_ref[0, hd, HEAD_DIM:, :] = ones


def _mem_kv(mem, gain, w):
    b, m, d = mem.shape
    n = w.shape[1]
    return pl.pallas_call(
        _mem_kv_kernel,
        grid=(b,),
        in_specs=[pl.BlockSpec((1, m, d), lambda i: (i, 0, 0)), _const_spec((1, d)), _const_spec((d, n))],
        out_specs=(pl.BlockSpec((1, m, CROSS_W), lambda i: (i, 0, 0)),
                   pl.BlockSpec((1, N_CROSS_HEADS, PV_ROWS, m), lambda i: (i, 0, 0, 0))),
        out_shape=(jax.ShapeDtypeStruct((b, m, CROSS_W), BF16),
                   jax.ShapeDtypeStruct((b, N_CROSS_HEADS, PV_ROWS, m), BF16)),
        compiler_params=_cparams("parallel"),
        name="mem_kv",
    )(mem, gain, w)


def _merge_kernel(x_ref, a_ref, y_ref, u_ref, qct_ref, gate_ref, mk_ref, mvt_ref, dskip_ref, wglu_ref,
                  pa_ref, ps_ref, pc_ref, wout_ref, o_ref):
    d = x_ref.shape[2]
    heads = []
    for h in range(N_CROSS_HEADS):
        mk = _head_lanes(mk_ref[0, :, (h // 2) * LANES:(h // 2 + 1) * LANES], h)
        st = jnp.dot(mk, qct_ref[0, h * HEAD_DIM:(h + 1) * HEAD_DIM, :], preferred_element_type=F32)
        pt = jnp.exp(st - jnp.max(st, axis=0, keepdims=True)).astype(BF16)
        ot = jnp.dot(mvt_ref[0, h], pt, preferred_element_type=F32)
        heads.append(ot[:HEAD_DIM] / ot[HEAD_DIM:HEAD_DIM + 1])
    cross = jnp.concatenate(heads, axis=0).T.astype(BF16)
    y = jnp.concatenate([y_ref[0, c].T for c in range(y_ref.shape[1])], axis=0)
    y = y + dskip_ref[...] * u_ref[0].astype(F32)
    y = jax.nn.gelu(y)
    y = y * _sigmoid(jnp.dot(y.astype(BF16), wglu_ref[...], preferred_element_type=F32))
    yb = y.astype(BF16)
    out = x_ref[0]
    half = d // 2
    for c0 in range(0, d, half):
        cols = slice(c0, c0 + half)
        gate = lambda j: gate_ref[0, :, j * d + c0:j * d + c0 + half].astype(F32)
        m = gate(0) * jnp.dot(a_ref[0], pa_ref[:, cols], preferred_element_type=F32)
        m = m + gate(1) * jnp.dot(yb, ps_ref[:, cols], preferred_element_type=F32)
        m = m + gate(2) * jnp.dot(cross, pc_ref[:, cols], preferred_element_type=F32)
        out = out + jnp.dot(m.astype(BF16), wout_ref[cols, :], preferred_element_type=F32)
    o_ref[0] = out


def _merge(x, attn, yscan, u, qct, gates, mk, mvt, d_skip, w_glu, p_attn, p_ssm, p_cross, w_out, tl):
    b, l, d = x.shape
    row = lambda width: pl.BlockSpec((1, tl, width), lambda i, j: (i, j, 0))
    col = lambda height: pl.BlockSpec((1, height, tl), lambda i, j: (i, 0, j))
    n_mem = mk.shape[1]
    return pl.pallas_call(
        _merge_kernel,
        grid=(b, l // tl),
        in_specs=[row(d), row(ATTN_W),
                  pl.BlockSpec((1, tl // S5_CHUNK, SSM_W, S5_CHUNK), lambda i, j: (i, j, 0, 0)),
                  row(SSM_W), col(CROSS_W), row(N_BRANCHES * d),
                  pl.BlockSpec((1, n_mem, CROSS_W), lambda i, j: (i, 0, 0)),
                  pl.BlockSpec((1, N_CROSS_HEADS, PV_ROWS, n_mem), lambda i, j: (i, 0, 0, 0)),
                  _const_spec((1, SSM_W)), _const_spec((SSM_W, SSM_W)),
                  _const_spec((ATTN_W, d)), _const_spec((SSM_W, d)), _const_spec((CROSS_W, d)),
                  _const_spec((d, d))],
        out_specs=row(d),
        out_shape=jax.ShapeDtypeStruct((b, l, d), F32),
        compiler_params=_cparams("parallel", "parallel"),
        name="merge",
    )(x, attn, yscan, u, qct, gates, mk, mvt, d_skip, w_glu, p_attn, p_ssm, p_cross, w_out)


MXU_TILE = 256


def _ffn_chunks(dff, tiles_per_chunk=11):
    step = tiles_per_chunk * MXU_TILE
    return [min(step, dff - c0) for c0 in range(0, dff, step)]


def _ffn_kernel(xp_ref, x_ref, xn_ref, g_ref, wup_ref, cw_ref, cb_ref, wdn_ref, gf_ref, o_ref,
                h_scr, gate_scr, *, final_norm):
    j = pl.program_id(1)
    nj = pl.num_programs(1)
    tm = x_ref.shape[1]
    dff = wdn_ref.shape[0]
    gain = g_ref[...]
    x = x_ref[0]
    h_scr[0:HALO, :] = jnp.where(j == 0, 0.0, _rms(xp_ref[0], gain)).astype(BF16)
    h_scr[HALO:HALO + tm, :] = _rms(x, gain).astype(BF16)
    h_scr[HALO + tm:, :] = jnp.where(j == nj - 1, 0.0, _rms(xn_ref[0], gain)).astype(BF16)
    y = x
    c0 = 0
    for wc in _ffn_chunks(dff):
        gs = gate_scr.at[:, c0:c0 + wc]
        gs[...] = jnp.dot(h_scr[...], wup_ref[:, c0:c0 + wc], preferred_element_type=F32)
        val = jnp.dot(h_scr[HALO:HALO + tm, :], wup_ref[:, dff + c0:dff + c0 + wc], preferred_element_type=F32)
        cw = cw_ref[:, c0:c0 + wc]
        gc = (gs[HALO - 1:HALO - 1 + tm, :] * cw[0:1, :] + gs[HALO:HALO + tm, :] * cw[1:2, :]
              + gs[HALO + 1:HALO + 1 + tm, :] * cw[2:3, :] + cb_ref[:, c0:c0 + wc])
        act = (jax.nn.gelu(gc) * val).astype(BF16)
        y = y + jnp.dot(act, wdn_ref[c0:c0 + wc, :], preferred_element_type=F32)
        c0 += wc
    if final_norm:
        y = _rms(y, gf_ref[...])
    o_ref[0] = y


def _ffn(x, gain, w_up, conv_w, conv_b, w_down, final_gain, final_norm, tm):
    b, l, d = x.shape
    dff = w_down.shape[0]
    r = tm // HALO
    nh = l // HALO
    return pl.pallas_call(
        functools.partial(_ffn_kernel, final_norm=final_norm),
        grid=(b, l // tm),
        in_specs=[pl.BlockSpec((1, HALO, d), lambda i, j: (i, jnp.maximum(j * r - 1, 0), 0)),
                  pl.BlockSpec((1, tm, d), lambda i, j: (i, j, 0)),
                  pl.BlockSpec((1, HALO, d), lambda i, j: (i, jnp.minimum((j + 1) * r, nh - 1), 0)),
                  _const_spec((1, d)), _const_spec((d, 2 * dff)), _const_spec((3, dff)),
                  _const_spec((1, dff)), _const_spec((dff, d)), _const_spec((1, d))],
        out_specs=pl.BlockSpec((1, tm, d), lambda i, j: (i, j, 0)),
        out_shape=jax.ShapeDtypeStruct((b, l, d), F32),
        scratch_shapes=[pltpu.VMEM((tm + 2 * HALO, d), BF16),
                        pltpu.VMEM((tm + 2 * HALO, dff), F32)],
        compiler_params=_cparams("parallel", "arbitrary"),
        name="ffn",
    )(x, x, x, gain, w_up, conv_w, conv_b, w_down, final_gain)


def _rope_tables(l):
    rows = l // GRID_W
    r = jnp.broadcast_to(jnp.arange(rows, dtype=F32)[:, None], (rows, GRID_W)).reshape(l)
    c = jnp.broadcast_to(jnp.arange(GRID_W, dtype=F32)[None, :], (rows, GRID_W)).reshape(l)
    freqs = ROPE_THETA ** (-jnp.arange(ROPE_PAIRS, dtype=F32) / ROPE_PAIRS)
    ang_r, ang_c = r[:, None] * freqs, c[:, None] * freqs
    ang = jnp.concatenate([ang_r, ang_r, ang_c, ang_c], axis=-1)
    cos, sin = jnp.cos(ang), jnp.sin(ang)
    first = (jnp.arange(HEAD_DIM) % (2 * ROPE_PAIRS)) < ROPE_PAIRS
    sin_up = jnp.where(first, -sin, 0.0)
    sin_dn = jnp.where(first, 0.0, sin)
    two = lambda a: jnp.concatenate([a, a], axis=1)
    return two(cos), two(sin_up), two(sin_dn)


def _prepare_layer(p, l):
    bf = lambda a: a.astype(BF16)
    row = lambda a: a.astype(F32).reshape(1, -1)
    max_abs = lambda a: jnp.max(jnp.abs(a.astype(F32)))
    score_bound = 1.02 * QK_SCALE * HEAD_DIM * max_abs(p['q_norm'][l]) * max_abs(p['k_norm'][l])
    return dict(
        norm_mix=row(p['norm_mix'][l]), w_in=bf(p['w_in'][l]),
        q_gain=row(jnp.tile(p['q_norm'][l], N_HEADS)), k_gain=row(jnp.tile(p['k_norm'][l], N_KV_HEADS)),
        score_bound=score_bound,
        s5=_s5_operators(p['ssm_a_re'][l], p['ssm_a_im'][l], p['ssm_log_dt'][l], p['ssm_b_re'][l],
                         p['ssm_b_im'][l], p['ssm_c_re'][l], p['ssm_c_im'][l], S5_CHUNK),
        d_skip=row(p['ssm_d'][l]), w_glu=bf(p['ssm_glu'][l]),
        mem_norm=row(p['mem_norm'][l]), w_mem_kv=bf(p['w_mem_kv'][l]),
        p_attn=bf(p['p_attn'][l]), p_ssm=bf(p['p_ssm'][l]), p_cross=bf(p['p_cross'][l]), w_out=bf(p['w_out'][l]),
        norm_ffn=row(p['norm_ffn'][l]), w_up=bf(p['w_up'][l]), conv_w=p['conv_w'][l].astype(F32),
        conv_b=row(p['conv_b'][l]), w_down=bf(p['w_down'][l]),
    )


def _encode(x, mem, layers, final_gain):
    b, l, d = x.shape
    tables = _rope_tables(l)
    tk = _pick(l, ROW_TILE)
    tq = _pick(l, QUERY_TILE)
    tm = _pick(l, ROW_TILE)
    for li, w in enumerate(layers):
        qt, k, vt, u, ut, qct, gates = _in_proj(x, w['norm_mix'], w['w_in'], tables, w['q_gain'], w['k_gain'], tk)
        attn = _attention(qt, k, vt, tq, w['score_bound'])
        yscan = _s5_scan_branch(ut, w['s5'])
        mk, mvt = _mem_kv(mem, w['mem_norm'], w['w_mem_kv'])
        x = _merge(x, attn, yscan, u, qct, gates, mk, mvt, w['d_skip'], w['w_glu'], w['p_attn'], w['p_ssm'],
                   w['p_cross'], w['w_out'], tm)
        x = _ffn(x, w['norm_ffn'], w['w_up'], w['conv_w'], w['conv_b'], w['w_down'], final_gain,
                 li == len(layers) - 1, tm)
    return x


def kernel(x_prompt, x_sample, mem_prompt, mem_sample, norm_mix, w_in, q_norm, k_norm, ssm_a_re, ssm_a_im, ssm_log_dt, ssm_b_re, ssm_b_im, ssm_c_re, ssm_c_im, ssm_d, ssm_glu, mem_norm, w_mem_kv, p_attn, p_ssm, p_cross, w_out, norm_ffn, w_up, conv_w, conv_b, w_down, norm_final):
    p = dict(norm_mix=norm_mix, w_in=w_in, q_norm=q_norm, k_norm=k_norm, ssm_a_re=ssm_a_re, ssm_a_im=ssm_a_im,
             ssm_log_dt=ssm_log_dt, ssm_b_re=ssm_b_re, ssm_b_im=ssm_b_im, ssm_c_re=ssm_c_re, ssm_c_im=ssm_c_im,
             ssm_d=ssm_d, ssm_glu=ssm_glu, mem_norm=mem_norm, w_mem_kv=w_mem_kv, p_attn=p_attn, p_ssm=p_ssm,
             p_cross=p_cross, w_out=w_out, norm_ffn=norm_ffn, w_up=w_up, conv_w=conv_w, conv_b=conv_b,
             w_down=w_down)
    layers = [_prepare_layer(p, l) for l in range(norm_mix.shape[0])]
    final_gain = norm_final.astype(F32).reshape(1, -1)
    y_prompt = _encode(x_prompt, mem_prompt, layers, final_gain)
    y_sample = _encode(x_sample, mem_sample, layers, final_gain)
    return (y_prompt, y_sample)
```

```python
import functools

import jax
import jax.numpy as jnp
from jax import lax
from jax.experimental import pallas as pl
from jax.experimental.pallas import tpu as pltpu

HEAD_DIM = 64
N_HEADS = 8
N_KV_HEADS = 2
KV_GROUP = N_HEADS // N_KV_HEADS
ATTN_W = N_HEADS * HEAD_DIM
KV_W = N_KV_HEADS * HEAD_DIM
SSM_GROUP_CH = 16
SSM_W = 256
SSM_GROUPS = SSM_W // SSM_GROUP_CH
SSM_STATE = 64
N_CROSS_HEADS = 4
CROSS_W = N_CROSS_HEADS * HEAD_DIM
N_BRANCHES = 3
GRID_W = 64
ROPE_THETA = 10000.0
ROPE_PAIRS = HEAD_DIM // 4
EPS = 1e-6

LOG2E = 1.4426950408889634
QK_SCALE = HEAD_DIM ** -0.5 * LOG2E
SAFE_SCORE = 64.0
LANES = 128
S5_CHUNK = LANES
PV_ROWS = 2 * HEAD_DIM
HALO = 16
ROW_TILE = 512
QUERY_TILE = 1024
VMEM_LIMIT = 56 * 1024 * 1024

F32 = jnp.float32
BF16 = jnp.bfloat16


def _cparams(*sem):
    return pltpu.CompilerParams(dimension_semantics=sem, vmem_limit_bytes=VMEM_LIMIT)


def _const_spec(shape):
    nd = len(shape)
    return pl.BlockSpec(shape, lambda *_: (0,) * nd, pipeline_mode=pl.Buffered(1))


def _rms(x, gain):
    return x * lax.rsqrt(jnp.mean(x * x, axis=-1, keepdims=True) + EPS) * gain


def _sigmoid(x):
    return 0.5 * jnp.tanh(0.5 * x) + 0.5


def _pick(n, pref):
    t = min(n, pref)
    while n % t:
        t //= 2
    return t


def _head_norm_rope(x, gain, cos, sin_up, sin_dn, out_scale):
    w = x.shape[1]
    reps = w // LANES
    tile = lambda a: a if reps == 1 else jnp.concatenate([a] * reps, axis=1)
    xg = x * gain
    y = (xg * tile(cos)
         + pltpu.roll(xg, w - ROPE_PAIRS, 1) * tile(sin_up)
         + pltpu.roll(xg, ROPE_PAIRS, 1) * tile(sin_dn))
    lane = lax.broadcasted_iota(jnp.int32, (1, LANES), 1)
    low = lane < HEAD_DIM
    outs = []
    for p in range(reps):
        xp = x[:, p * LANES:(p + 1) * LANES]
        sq = xp * xp
        ss_all = jnp.sum(sq, axis=-1, keepdims=True)
        ss_lo = jnp.sum(jnp.where(low, sq, 0.0), axis=-1, keepdims=True)
        r_lo = lax.rsqrt(ss_lo * (1.0 / HEAD_DIM) + EPS) * out_scale
        r_hi = lax.rsqrt((ss_all - ss_lo) * (1.0 / HEAD_DIM) + EPS) * out_scale
        outs.append(y[:, p * LANES:(p + 1) * LANES] * jnp.where(low, r_lo, r_hi))
    return outs[0] if reps == 1 else jnp.concatenate(outs, axis=1)


def _in_proj_kernel(x_ref, g_ref, w_ref, cos_ref, sup_ref, sdn_ref, qg_ref, kg_ref,
                    qt_ref, k_ref, vt_ref, u_ref, ut_ref, qc_ref, gate_ref, h_scr):
    h_scr[...] = _rms(x_ref[0], g_ref[...]).astype(BF16)
    z = jnp.dot(h_scr[...], w_ref[...], preferred_element_type=F32)

    def seg(a, b):
        return z[:, a:b]

    cos, sup, sdn = cos_ref[...], sup_ref[...], sdn_ref[...]
    o = 0
    q = seg(o, o + ATTN_W)
    qt_ref[0] = _head_norm_rope(q, qg_ref[...], cos, sup, sdn, QK_SCALE).T.astype(BF16)
    o += ATTN_W
    kv = seg(o, o + 2 * KV_W)
    k_ref[0] = _head_norm_rope(kv[:, :KV_W], kg_ref[...], cos, sup, sdn, 1.0).astype(BF16)
    vt = kv[:, KV_W:].T.astype(BF16)
    ones = jnp.ones((PV_ROWS - HEAD_DIM, vt.shape[1]), BF16)
    for g in range(N_KV_HEADS):
        vt_ref[0, 0, g, :HEAD_DIM, :] = vt[g * HEAD_DIM:(g + 1) * HEAD_DIM, :]
        vt_ref[0, 0, g, HEAD_DIM:, :] = ones
    o += 2 * KV_W
    u = seg(o, o + SSM_W)
    u_ref[0] = u.astype(BF16)
    ut = u.T
    for c in range(ut_ref.shape[1]):
        ut_ref[0, c] = ut[:, c * S5_CHUNK:(c + 1) * S5_CHUNK]
    o += SSM_W
    qc_ref[0] = (seg(o, o + CROSS_W) * HEAD_DIM ** -0.5).T.astype(BF16)
    o += CROSS_W
    d = x_ref.shape[2]
    for j in range(N_BRANCHES):
        gate_ref[0, :, j * d:(j + 1) * d] = _sigmoid(seg(o + j * d, o + (j + 1) * d)).astype(BF16)


def _in_proj(x, gain, w, tables, q_gain, k_gain, tk):
    b, l, d = x.shape
    n_in = w.shape[1]
    nj = l // tk
    row = lambda width: pl.BlockSpec((1, tk, width), lambda i, j: (i, j, 0))
    tab = pl.BlockSpec((tk, LANES), lambda i, j: (j, 0))
    out_shape = (
        jax.ShapeDtypeStruct((b, ATTN_W, l), BF16),
        jax.ShapeDtypeStruct((b, l, KV_W), BF16),
        jax.ShapeDtypeStruct((b, nj, N_KV_HEADS, PV_ROWS, tk), BF16),
        jax.ShapeDtypeStruct((b, l, SSM_W), BF16),
        jax.ShapeDtypeStruct((b, l // S5_CHUNK, SSM_W, S5_CHUNK), F32),
        jax.ShapeDtypeStruct((b, CROSS_W, l), BF16),
        jax.ShapeDtypeStruct((b, l, N_BRANCHES * d), BF16),
    )
    col = lambda height: pl.BlockSpec((1, height, tk), lambda i, j: (i, 0, j))
    return pl.pallas_call(
        _in_proj_kernel,
        grid=(b, nj),
        in_specs=[row(d), _const_spec((1, d)), _const_spec((d, n_in)), tab, tab, tab,
                  _const_spec((1, ATTN_W)), _const_spec((1, KV_W))],
        out_specs=(col(ATTN_W), row(KV_W),
                   pl.BlockSpec((1, 1, N_KV_HEADS, PV_ROWS, tk), lambda i, j: (i, j, 0, 0, 0)),
                   row(SSM_W), pl.BlockSpec((1, tk // S5_CHUNK, SSM_W, S5_CHUNK), lambda i, j: (i, j, 0, 0)),
                   col(CROSS_W), row(N_BRANCHES * d)),
        out_shape=out_shape,
        scratch_shapes=[pltpu.VMEM((tk, d), BF16)],
        compiler_params=_cparams("parallel", "parallel"),
        name="in_proj",
    )(x, gain, w, *tables, q_gain, k_gain)


def _head_lanes(pair, h):
    return pair[:, (h % 2) * HEAD_DIM:(h % 2 + 1) * HEAD_DIM]


def _attn_kernel(qt_ref, k_ref, vt_ref, o_ref, qs_ref, acc_ref, m_ref, *, bounded):
    tq = qt_ref.shape[2]
    nc, tk = vt_ref.shape[1], vt_ref.shape[4]
    qs_ref[...] = jnp.zeros(qs_ref.shape, BF16)
    for h in range(N_HEADS):
        g, r = divmod(h, KV_GROUP)
        qs_ref[g, g * HEAD_DIM:(g + 1) * HEAD_DIM, r * tq:(r + 1) * tq] = qt_ref[0, h * HEAD_DIM:(h + 1) * HEAD_DIM, :]
    acc_ref[...] = jnp.zeros(acc_ref.shape, F32)
    if not bounded:
        m_ref[...] = jnp.full(m_ref.shape, -jnp.inf, F32)

    def body(c, carry):
        kc = k_ref[0, pl.ds(pl.multiple_of(c * tk, tk), tk), :]
        for g in range(N_KV_HEADS):
            st = jnp.dot(kc, qs_ref[g], preferred_element_type=F32)
            if bounded:
                pt = jnp.exp2(st).astype(BF16)
                acc_ref[g] += jnp.dot(vt_ref[0, c, g], pt, preferred_element_type=F32)
            else:
                m_old = m_ref[g]
                m_new = jnp.maximum(m_old, jnp.max(st, axis=0, keepdims=True))
                pt = jnp.exp2(st - m_new).astype(BF16)
                acc_ref[g] = (jnp.exp2(m_old - m_new) * acc_ref[g]
                              + jnp.dot(vt_ref[0, c, g], pt, preferred_element_type=F32))
                m_ref[g] = m_new
        return carry

    lax.fori_loop(0, nc, body, 0, unroll=min(nc, 8))
    for h in range(0, N_HEADS, 2):
        g, r = divmod(h, KV_GROUP)
        acc = acc_ref[g, :, r * tq:(r + 2) * tq]
        ot = acc[:HEAD_DIM] / acc[HEAD_DIM:HEAD_DIM + 1]
        pair = jnp.concatenate([ot[:, :tq], ot[:, tq:]], axis=0)
        o_ref[0, :, h * HEAD_DIM:(h + 2) * HEAD_DIM] = pair.T.astype(BF16)


def _attention(qt, k, vt, tq, score_bound):
    b, _, l = qt.shape
    nc, tk = vt.shape[1], vt.shape[4]

    def call(bounded):
        return pl.pallas_call(
            functools.partial(_attn_kernel, bounded=bounded),
            grid=(b, l // tq),
            in_specs=[pl.BlockSpec((1, ATTN_W, tq), lambda i, j: (i, 0, j)),
                      pl.BlockSpec((1, l, KV_W), lambda i, j: (i, 0, 0)),
                      pl.BlockSpec((1, nc, N_KV_HEADS, PV_ROWS, tk), lambda i, j: (i, 0, 0, 0, 0))],
            out_specs=pl.BlockSpec((1, tq, ATTN_W), lambda i, j: (i, j, 0)),
            out_shape=jax.ShapeDtypeStruct((b, l, ATTN_W), BF16),
            scratch_shapes=[pltpu.VMEM((N_KV_HEADS, KV_W, KV_GROUP * tq), BF16),
                            pltpu.VMEM((N_KV_HEADS, PV_ROWS, KV_GROUP * tq), F32),
                            pltpu.VMEM((N_KV_HEADS, 1, KV_GROUP * tq), F32)],
            compiler_params=_cparams("parallel", "parallel"),
            name="attention" if bounded else "attention_running_max",
        )

    return lax.cond(score_bound <= SAFE_SCORE, call(True), call(False), qt, k, vt)


def _complex_powers(lr, li, n):
    pr, pi = jnp.ones_like(lr)[None], jnp.zeros_like(li)[None]
    cr, ci = lr, li
    while pr.shape[0] < n:
        pr, pi = (jnp.concatenate([pr, pr * cr - pi * ci]), jnp.concatenate([pi, pr * ci + pi * cr]))
        cr, ci = cr * cr - ci * ci, 2.0 * cr * ci
    return pr[:n], pi[:n]


def _toeplitz_kernel(k_ref, o_ref):
    n, hch, t = k_ref.shape[1], k_ref.shape[2], k_ref.shape[3] // 2
    for i in range(n):
        for h in range(hch):
            rows = jnp.broadcast_to(k_ref[0, i, h:h + 1, :], (t, 2 * t))
            rolled = pltpu.roll(rows, t + 1, 1, stride=1, stride_axis=0)
            o_ref[0, i * t:(i + 1) * t, h * t:(h + 1) * t] = rolled[:, :t].astype(BF16)


def _toeplitz(k_cat):
    g, hin, hout, t2 = k_cat.shape
    t = t2 // 2
    n = _pick(hin, 4)
    return pl.pallas_call(
        _toeplitz_kernel,
        grid=(g, hin // n),
        in_specs=[pl.BlockSpec((1, n, hout, t2), lambda i, j: (i, j, 0, 0))],
        out_specs=pl.BlockSpec((1, n * t, hout * t), lambda i, j: (i, j, 0)),
        out_shape=jax.ShapeDtypeStruct((g, hin * t, hout * t), BF16),
        compiler_params=_cparams("parallel", "parallel"),
        name="s5_toeplitz",
    )(k_cat)


def _s5_operators(a_re, a_im, log_dt, b_re, b_im, c_re, c_im, t):
    hi = lax.Precision.HIGHEST
    a_re, a_im, log_dt = a_re.astype(F32), a_im.astype(F32), log_dt.astype(F32)
    b_re, b_im, c_re, c_im = (z.astype(F32) for z in (b_re, b_im, c_re, c_im))
    dt = jnp.exp(log_dt)[..., None]
    mag = jnp.exp(a_re * dt)
    lam_re, lam_im = mag * jnp.cos(a_im * dt), mag * jnp.sin(a_im * dt)
    num_re = lam_re - 1.0
    den = a_re * a_re + a_im * a_im
    coef_re = (num_re * a_re + lam_im * a_im) / den
    coef_im = (lam_im * a_re - num_re * a_im) / den
    pw_re, pw_im = _complex_powers(lam_re, lam_im, t + 1)
    e_re = pw_re * coef_re - pw_im * coef_im
    e_im = pw_re * coef_im + pw_im * coef_re
    g, p, hch = b_re.shape

    bt = lambda z: z.transpose(0, 2, 1)[:, :, None, :]
    cb_re = (c_re[:, None] * bt(b_re) - c_im[:, None] * bt(b_im)).reshape(g, hch * hch, p)
    cb_im = (c_re[:, None] * bt(b_im) + c_im[:, None] * bt(b_re)).reshape(g, hch * hch, p)
    kern = (jnp.einsum('gkp,tdgp->dgkt', cb_re, e_re[:t], precision=hi)
            - jnp.einsum('gkp,tdgp->dgkt', cb_im, e_im[:t], precision=hi))
    k_f, k_b = kern[0], kern[1]
    k_cat = jnp.concatenate([k_b[..., :0:-1], k_f[..., :1] + k_b[..., :1], k_f[..., 1:],
                             jnp.zeros_like(k_f[..., :1])], axis=-1)
    toep = _toeplitz(k_cat.reshape(g, hch, hch, 2 * t))

    bt_re, bt_im = b_re.transpose(0, 2, 1)[:, :, None, :], b_im.transpose(0, 2, 1)[:, :, None, :]

    def times_b(er, ei):
        er, ei = er.transpose(1, 0, 2)[:, None], ei.transpose(1, 0, 2)[:, None]
        re = er * bt_re - ei * bt_im
        im = er * bt_im + ei * bt_re
        f = lambda z: z.reshape(g, hch * t, p)
        return f(re), f(im)
    sf_re, sf_im = times_b(e_re[:t, 0][::-1], e_im[:t, 0][::-1])
    sb_re, sb_im = times_b(e_re[:t, 1], e_im[:t, 1])
    w_s = jnp.concatenate([sf_re, sf_im, sb_re, sb_im], axis=2)

    ct_re, ct_im = c_re.transpose(0, 2, 1)[:, :, :, None], c_im.transpose(0, 2, 1)[:, :, :, None]

    def c_times(pr, pi):
        pr, pi = pr.transpose(1, 2, 0)[:, :, None, :], pi.transpose(1, 2, 0)[:, :, None, :]
        re = ct_re * pr - ct_im * pi
        im = ct_re * pi + ct_im * pr
        f = lambda z: z.reshape(g, p, hch * t)
        return f(re), f(-im)
    of_re, of_im = c_times(pw_re[1:t + 1, 0], pw_im[1:t + 1, 0])
    ob_re, ob_im = c_times(pw_re[1:t + 1, 1][::-1], pw_im[1:t + 1, 1][::-1])
    w_o = jnp.concatenate([of_re, of_im, ob_re, ob_im], axis=1)

    lt_re, lt_im = pw_re[t], pw_im[t]
    a1 = jnp.concatenate([lt_re[0], lt_re[0], lt_re[1], lt_re[1]], axis=-1)[:, None, :]
    a2 = jnp.concatenate([-lt_im[0], lt_im[0], -lt_im[1], lt_im[1]], axis=-1)[:, None, :]
    return toep.astype(BF16), w_s.astype(BF16), w_o.astype(BF16), a1, a2


def _s5_chunk_rows(u_ref):
    return jnp.concatenate([u_ref[:, h, :] for h in range(u_ref.shape[1])], axis=1).astype(BF16)


def _s5_state_kernel(u_ref, ws_ref, s_ref):
    s_ref[0] = jnp.dot(_s5_chunk_rows(u_ref), ws_ref[0], preferred_element_type=F32)


def _s5_state(ut, w_s):
    r, ch, t = ut.shape
    g, k, n = w_s.shape
    return pl.pallas_call(
        _s5_state_kernel,
        grid=(g,),
        in_specs=[pl.BlockSpec((r, ch // g, t), lambda i: (0, i, 0)),
                  pl.BlockSpec((1, k, n), lambda i: (i, 0, 0))],
        out_specs=pl.BlockSpec((1, r, n), lambda i: (i, 0, 0)),
        out_shape=jax.ShapeDtypeStruct((g, r, n), F32),
        compiler_params=_cparams("parallel"),
        name="s5_state",
    )(ut, w_s)


def _s5_scan_kernel(s_ref, a1_ref, a2_ref, h_ref, *, nb):
    gb, rows, _ = s_ref.shape
    nchunk = rows // (2 * nb)
    half = 2 * SSM_STATE
    a1, a2 = a1_ref[...], a2_ref[...]
    a1f, a1b, a2f, a2b = a1[:, :, :half], a1[:, :, half:], a2[:, :, :half], a2[:, :, half:]

    def swap(z):
        return pltpu.roll(z.reshape(gb * nb, half), SSM_STATE, 1).reshape(gb, nb, half)

    def body(j, carry):
        hf, hb = carry
        rf = pl.ds(2 * j, nb, stride=2 * nchunk)
        rb = pl.ds(2 * (nchunk - 1 - j) + 1, nb, stride=2 * nchunk)
        h_ref[:, rf, :] = hf
        h_ref[:, rb, :] = hb
        hf = a1f * hf + a2f * swap(hf) + s_ref[:, rf, :]
        hb = a1b * hb + a2b * swap(hb) + s_ref[:, rb, :]
        return hf, hb

    zero = jnp.zeros((gb, nb, half), F32)
    lax.fori_loop(0, nchunk, body, (zero, zero))


def _s5_scan(s, a1, a2, nb, gb):
    g, r, n = s.shape
    half = n // 2
    blk = pl.BlockSpec((gb, 2 * r, half), lambda i: (i, 0, 0))
    par = pl.BlockSpec((gb, 1, n), lambda i: (i, 0, 0))
    return pl.pallas_call(
        functools.partial(_s5_scan_kernel, nb=nb),
        grid=(g // gb,),
        in_specs=[blk, par, par],
        out_specs=blk,
        out_shape=jax.ShapeDtypeStruct((g, 2 * r, half), F32),
        compiler_params=_cparams("parallel"),
        name="s5_scan",
    )(s.reshape(g, 2 * r, half), a1, a2).reshape(g, r, n)


def _s5_out_kernel(u_ref, h_ref, toep_ref, wo_ref, y_ref):
    _, hch, t = u_ref.shape
    y = jnp.dot(_s5_chunk_rows(u_ref), toep_ref[0], preferred_element_type=F32)
    y = y + jnp.dot(h_ref[0].astype(BF16), wo_ref[0], preferred_element_type=F32)
    for h in range(hch):
        y_ref[:, h, :] = y[:, h * t:(h + 1) * t]


def _s5_out(ut, hin, toep, w_o):
    r, ch, t = ut.shape
    g, n, k = w_o.shape
    blk = pl.BlockSpec((r, ch // g, t), lambda i: (0, i, 0))
    return pl.pallas_call(
        _s5_out_kernel,
        grid=(g,),
        in_specs=[blk, pl.BlockSpec((1, r, n), lambda i: (i, 0, 0)),
                  pl.BlockSpec((1, k, k), lambda i: (i, 0, 0)),
                  pl.BlockSpec((1, n, k), lambda i: (i, 0, 0))],
        out_specs=blk,
        out_shape=jax.ShapeDtypeStruct((r, ch, t), F32),
        compiler_params=_cparams("parallel"),
        name="s5_out",
    )(ut, hin, toep, w_o)


def _s5_scan_branch(ut, ops):
    toep, w_s, w_o, a1, a2 = ops
    b, c, ch, t = ut.shape
    ut = ut.reshape(b * c, ch, t)
    s = _s5_state(ut, w_s)
    hin = _s5_scan(s, a1, a2, b, 4)
    return _s5_out(ut, hin, toep, w_o).reshape(b, c, ch, t)


def _mem_kv_kernel(m_ref, g_ref, w_ref, mk_ref, mvt_ref):
    h = _rms(m_ref[0], g_ref[...]).astype(BF16)
    kv = jnp.dot(h, w_ref[...], preferred_element_type=F32)
    mk_ref[0] = kv[:, :CROSS_W].astype(BF16)
    vt = kv[:, CROSS_W:].T.astype(BF16)
    ones = jnp.ones((PV_ROWS - HEAD_DIM, vt.shape[1]), BF16)
    for hd in range(N_CROSS_HEADS):
        mvt_ref[0, hd, :HEAD_DIM, :] = vt[hd * HEAD_DIM:(hd + 1) * HEAD_DIM, :]
        mvt_ref[0, hd, HEAD_DIM:, :] = ones


def _mem_kv(mem, gain, w):
    b, m, d = mem.shape
    n = w.shape[1]
    return pl.pallas_call(
        _mem_kv_kernel,
        grid=(b,),
        in_specs=[pl.BlockSpec((1, m, d), lambda i: (i, 0, 0)), _const_spec((1, d)), _const_spec((d, n))],
        out_specs=(pl.BlockSpec((1, m, CROSS_W), lambda i: (i, 0, 0)),
                   pl.BlockSpec((1, N_CROSS_HEADS, PV_ROWS, m), lambda i: (i, 0, 0, 0))),
        out_shape=(jax.ShapeDtypeStruct((b, m, CROSS_W), BF16),
                   jax.ShapeDtypeStruct((b, N_CROSS_HEADS, PV_ROWS, m), BF16)),
        compiler_params=_cparams("parallel"),
        name="mem_kv",
    )(mem, gain, w)


def _merge_kernel(x_ref, a_ref, y_ref, u_ref, qct_ref, gate_ref, mk_ref, mvt_ref, dskip_ref, wglu_ref,
                  pa_ref, ps_ref, pc_ref, wout_ref, o_ref):
    d = x_ref.shape[2]
    heads = []
    for h in range(N_CROSS_HEADS):
        mk = _head_lanes(mk_ref[0, :, (h // 2) * LANES:(h // 2 + 1) * LANES], h)
        st = jnp.dot(mk, qct_ref[0, h * HEAD_DIM:(h + 1) * HEAD_DIM, :], preferred_element_type=F32)
        pt = jnp.exp(st - jnp.max(st, axis=0, keepdims=True)).astype(BF16)
        ot = jnp.dot(mvt_ref[0, h], pt, preferred_element_type=F32)
        heads.append(ot[:HEAD_DIM] / ot[HEAD_DIM:HEAD_DIM + 1])
    cross = jnp.concatenate(heads, axis=0).T.astype(BF16)
    y = jnp.concatenate([y_ref[0, c].T for c in range(y_ref.shape[1])], axis=0)
    y = y + dskip_ref[...] * u_ref[0].astype(F32)
    y = jax.nn.gelu(y)
    y = y * _sigmoid(jnp.dot(y.astype(BF16), wglu_ref[...], preferred_element_type=F32))
    yb = y.astype(BF16)
    out = x_ref[0]
    half = d // 2
    for c0 in range(0, d, half):
        cols = slice(c0, c0 + half)
        gate = lambda j: gate_ref[0, :, j * d + c0:j * d + c0 + half].astype(F32)
        m = gate(0) * jnp.dot(a_ref[0], pa_ref[:, cols], preferred_element_type=F32)
        m = m + gate(1) * jnp.dot(yb, ps_ref[:, cols], preferred_element_type=F32)
        m = m + gate(2) * jnp.dot(cross, pc_ref[:, cols], preferred_element_type=F32)
        out = out + jnp.dot(m.astype(BF16), wout_ref[cols, :], preferred_element_type=F32)
    o_ref[0] = out


def _merge(x, attn, yscan, u, qct, gates, mk, mvt, d_skip, w_glu, p_attn, p_ssm, p_cross, w_out, tl):
    b, l, d = x.shape
    row = lambda width: pl.BlockSpec((1, tl, width), lambda i, j: (i, j, 0))
    col = lambda height: pl.BlockSpec((1, height, tl), lambda i, j: (i, 0, j))
    n_mem = mk.shape[1]
    return pl.pallas_call(
        _merge_kernel,
        grid=(b, l // tl),
        in_specs=[row(d), row(ATTN_W),
                  pl.BlockSpec((1, tl // S5_CHUNK, SSM_W, S5_CHUNK), lambda i, j: (i, j, 0, 0)),
                  row(SSM_W), col(CROSS_W), row(N_BRANCHES * d),
                  pl.BlockSpec((1, n_mem, CROSS_W), lambda i, j: (i, 0, 0)),
                  pl.BlockSpec((1, N_CROSS_HEADS, PV_ROWS, n_mem), lambda i, j: (i, 0, 0, 0)),
                  _const_spec((1, SSM_W)), _const_spec((SSM_W, SSM_W)),
                  _const_spec((ATTN_W, d)), _const_spec((SSM_W, d)), _const_spec((CROSS_W, d)),
                  _const_spec((d, d))],
        out_specs=row(d),
        out_shape=jax.ShapeDtypeStruct((b, l, d), F32),
        compiler_params=_cparams("parallel", "parallel"),
        name="merge",
    )(x, attn, yscan, u, qct, gates, mk, mvt, d_skip, w_glu, p_attn, p_ssm, p_cross, w_out)


MXU_TILE = 256


def _ffn_chunks(dff, tiles_per_chunk=11):
    step = tiles_per_chunk * MXU_TILE
    return [min(step, dff - c0) for c0 in range(0, dff, step)]


def _ffn_kernel(xp_ref, x_ref, xn_ref, g_ref, wup_ref, cw_ref, cb_ref, wdn_ref, gf_ref, o_ref,
                h_scr, gate_scr, *, final_norm):
    j = pl.program_id(1)
    nj = pl.num_programs(1)
    tm = x_ref.shape[1]
    dff = wdn_ref.shape[0]
    gain = g_ref[...]
    x = x_ref[0]
    h_scr[0:HALO, :] = jnp.where(j == 0, 0.0, _rms(xp_ref[0], gain)).astype(BF16)
    h_scr[HALO:HALO + tm, :] = _rms(x, gain).astype(BF16)
    h_scr[HALO + tm:, :] = jnp.where(j == nj - 1, 0.0, _rms(xn_ref[0], gain)).astype(BF16)
    y = x
    c0 = 0
    for wc in _ffn_chunks(dff):
        gs = gate_scr.at[:, c0:c0 + wc]
        gs[...] = jnp.dot(h_scr[...], wup_ref[:, c0:c0 + wc], preferred_element_type=F32)
        val = jnp.dot(h_scr[HALO:HALO + tm, :], wup_ref[:, dff + c0:dff + c0 + wc], preferred_element_type=F32)
        cw = cw_ref[:, c0:c0 + wc]
        gc = (gs[HALO - 1:HALO - 1 + tm, :] * cw[0:1, :] + gs[HALO:HALO + tm, :] * cw[1:2, :]
              + gs[HALO + 1:HALO + 1 + tm, :] * cw[2:3, :] + cb_ref[:, c0:c0 + wc])
        act = (jax.nn.gelu(gc) * val).astype(BF16)
        y = y + jnp.dot(act, wdn_ref[c0:c0 + wc, :], preferred_element_type=F32)
        c0 += wc
    if final_norm:
        y = _rms(y, gf_ref[...])
    o_ref[0] = y


def _ffn(x, gain, w_up, conv_w, conv_b, w_down, final_gain, final_norm, tm):
    b, l, d = x.shape
    dff = w_down.shape[0]
    r = tm // HALO
    nh = l // HALO
    return pl.pallas_call(
        functools.partial(_ffn_kernel, final_norm=final_norm),
        grid=(b, l // tm),
        in_specs=[pl.BlockSpec((1, HALO, d), lambda i, j: (i, jnp.maximum(j * r - 1, 0), 0)),
                  pl.BlockSpec((1, tm, d), lambda i, j: (i, j, 0)),
                  pl.BlockSpec((1, HALO, d), lambda i, j: (i, jnp.minimum((j + 1) * r, nh - 1), 0)),
                  _const_spec((1, d)), _const_spec((d, 2 * dff)), _const_spec((3, dff)),
                  _const_spec((1, dff)), _const_spec((dff, d)), _const_spec((1, d))],
        out_specs=pl.BlockSpec((1, tm, d), lambda i, j: (i, j, 0)),
        out_shape=jax.ShapeDtypeStruct((b, l, d), F32),
        scratch_shapes=[pltpu.VMEM((tm + 2 * HALO, d), BF16),
                        pltpu.VMEM((tm + 2 * HALO, dff), F32)],
        compiler_params=_cparams("parallel", "arbitrary"),
        name="ffn",
    )(x, x, x, gain, w_up, conv_w, conv_b, w_down, final_gain)


def _rope_tables(l):
    rows = l // GRID_W
    r = jnp.broadcast_to(jnp.arange(rows, dtype=F32)[:, None], (rows, GRID_W)).reshape(l)
    c = jnp.broadcast_to(jnp.arange(GRID_W, dtype=F32)[None, :], (rows, GRID_W)).reshape(l)
    freqs = ROPE_THETA ** (-jnp.arange(ROPE_PAIRS, dtype=F32) / ROPE_PAIRS)
    ang_r, ang_c = r[:, None] * freqs, c[:, None] * freqs
    ang = jnp.concatenate([ang_r, ang_r, ang_c, ang_c], axis=-1)
    cos, sin = jnp.cos(ang), jnp.sin(ang)
    first = (jnp.arange(HEAD_DIM) % (2 * ROPE_PAIRS)) < ROPE_PAIRS
    sin_up = jnp.where(first, -sin, 0.0)
    sin_dn = jnp.where(first, 0.0, sin)
    two = lambda a: jnp.concatenate([a, a], axis=1)
    return two(cos), two(sin_up), two(sin_dn)


def _prepare_layer(p, l):
    bf = lambda a: a.astype(BF16)
    row = lambda a: a.astype(F32).reshape(1, -1)
    max_abs = lambda a: jnp.max(jnp.abs(a.astype(F32)))
    score_bound = 1.02 * QK_SCALE * HEAD_DIM * max_abs(p['q_norm'][l]) * max_abs(p['k_norm'][l])
    return dict(
        norm_mix=row(p['norm_mix'][l]), w_in=bf(p['w_in'][l]),
        q_gain=row(jnp.tile(p['q_norm'][l], N_HEADS)), k_gain=row(jnp.tile(p['k_norm'][l], N_KV_HEADS)),
        score_bound=score_bound,
        s5=_s5_operators(p['ssm_a_re'][l], p['ssm_a_im'][l], p['ssm_log_dt'][l], p['ssm_b_re'][l],
                         p['ssm_b_im'][l], p['ssm_c_re'][l], p['ssm_c_im'][l], S5_CHUNK),
        d_skip=row(p['ssm_d'][l]), w_glu=bf(p['ssm_glu'][l]),
        mem_norm=row(p['mem_norm'][l]), w_mem_kv=bf(p['w_mem_kv'][l]),
        p_attn=bf(p['p_attn'][l]), p_ssm=bf(p['p_ssm'][l]), p_cross=bf(p['p_cross'][l]), w_out=bf(p['w_out'][l]),
        norm_ffn=row(p['norm_ffn'][l]), w_up=bf(p['w_up'][l]), conv_w=p['conv_w'][l].astype(F32),
        conv_b=row(p['conv_b'][l]), w_down=bf(p['w_down'][l]),
    )


def _encode(x, mem, layers, final_gain):
    b, l, d = x.shape
    tables = _rope_tables(l)
    tk = _pick(l, ROW_TILE)
    tq = _pick(l, QUERY_TILE)
    tm = _pick(l, ROW_TILE)
    for li, w in enumerate(layers):
        qt, k, vt, u, ut, qct, gates = _in_proj(x, w['norm_mix'], w['w_in'], tables, w['q_gain'], w['k_gain'], tk)
        attn = _attention(qt, k, vt, tq, w['score_bound'])
        yscan = _s5_scan_branch(ut, w['s5'])
        mk, mvt = _mem_kv(mem, w['mem_norm'], w['w_mem_kv'])
        x = _merge(x, attn, yscan, u, qct, gates, mk, mvt, w['d_skip'], w['w_glu'], w['p_attn'], w['p_ssm'],
                   w['p_cross'], w['w_out'], tm)
        x = _ffn(x, w['norm_ffn'], w['w_up'], w['conv_w'], w['conv_b'], w['w_down'], final_gain,
                 li == len(layers) - 1, tm)
    return x


def kernel(x_prompt, x_sample, mem_prompt, mem_sample, norm_mix, w_in, q_norm, k_norm, ssm_a_re, ssm_a_im, ssm_log_dt, ssm_b_re, ssm_b_im, ssm_c_re, ssm_c_im, ssm_d, ssm_glu, mem_norm, w_mem_kv, p_attn, p_ssm, p_cross, w_out, norm_ffn, w_up, conv_w, conv_b, w_down, norm_final):
    p = dict(norm_mix=norm_mix, w_in=w_in, q_norm=q_norm, k_norm=k_norm, ssm_a_re=ssm_a_re, ssm_a_im=ssm_a_im,
             ssm_log_dt=ssm_log_dt, ssm_b_re=ssm_b_re, ssm_b_im=ssm_b_im, ssm_c_re=ssm_c_re, ssm_c_im=ssm_c_im,
             ssm_d=ssm_d, ssm_glu=ssm_glu, mem_norm=mem_norm, w_mem_kv=w_mem_kv, p_attn=p_attn, p_ssm=p_ssm,
             p_cross=p_cross, w_out=w_out, norm_ffn=norm_ffn, w_up=w_up, conv_w=conv_w, conv_b=conv_b,
             w_down=w_down)
    layers = [_prepare_layer(p, l) for l in range(norm_mix.shape[0])]
    final_gain = norm_final.astype(F32).reshape(1, -1)
    y_prompt = _encode(x_prompt, mem_prompt, layers, final_gain)
    y_sample = _encode(x_sample, mem_sample, layers, final_gain)
    return (y_prompt, y_sample)
```
